```python
import math
import jax, jax.numpy as jnp
from jax import lax
import numpy as np

D_MODEL = 2048
BATCH = 2
SEQ = 8192
DEPTH = 4

N_MEM = 256
HEAD_DIM = 128
MEM_HEADS = 4
MEM_WIDTH = MEM_HEADS * HEAD_DIM
LRU_WIDTH = D_MODEL - MEM_WIDTH
LRU_BLOCKS = LRU_WIDTH // HEAD_DIM
CONV_WIDTH = 4
LRU_C = 8.0
DIL_GROUPS = ((128, 1), (512, 4), (2048, 16))
DIL_WIDTH = D_MODEL - MEM_WIDTH
DIL_HEADS = DIL_WIDTH // HEAD_DIM
HEADS_PER_GROUP = DIL_HEADS // len(DIL_GROUPS)
D_FF = -(-8 * D_MODEL // (3 * 256)) * 256
Q_BLOCK = 128
N_A = DEPTH // 2
N_B = DEPTH - N_A
RMS_EPS = 1e-6
NEG_INF = -1e30

kernel_name = "yoco_rglru_dilated_swa_hybrid"


def rms_norm(x, g):
    xf = x.astype(jnp.float32)
    y = xf * lax.rsqrt(jnp.mean(xf * xf, axis=-1, keepdims=True) + RMS_EPS)
    return (y * g.astype(jnp.float32)).astype(x.dtype)


def swiglu_sublayer(h, pre_g, post_g, w_ffn_in, w_ffn_out):
    hn = rms_norm(h, pre_g)
    gu = hn @ w_ffn_in
    g, u = gu[..., :D_FF], gu[..., D_FF:]
    y = (jax.nn.silu(g) * u) @ w_ffn_out
    return h + rms_norm(y, post_g)


def memory_attention(q_flat, mem_n, w_mem_kv):
    bsz, s, _ = q_flat.shape
    q = q_flat.reshape(bsz, s, MEM_HEADS, HEAD_DIM)
    kv = mem_n @ w_mem_kv
    m = kv.shape[1]
    k = kv[..., :MEM_WIDTH].reshape(bsz, m, MEM_HEADS, HEAD_DIM)
    v = kv[..., MEM_WIDTH:].reshape(bsz, m, MEM_HEADS, HEAD_DIM)
    sc = jnp.einsum('bshd,bmhd->bhsm', q, k).astype(jnp.float32) * (HEAD_DIM ** -0.5)
    p = jax.nn.softmax(sc, axis=-1).astype(v.dtype)
    o = jnp.einsum('bhsm,bmhd->bshd', p, v)
    return o.reshape(bsz, s, MEM_WIDTH)


def causal_depthwise_conv(x, w, b):
    c = x.shape[-1]
    y = lax.conv_general_dilated(x, w[:, None, :], window_strides=(1,),
                                 padding=((CONV_WIDTH - 1, 0),),
                                 dimension_numbers=('NWC', 'WIO', 'NWC'),
                                 feature_group_count=c)
    return y + b


def _linear_recurrence_combine(c1, c2):
    a1, b1 = c1
    a2, b2 = c2
    return a1 * a2, a2 * b1 + b2


def rg_lru(x, w_a, b_a, w_x, b_x, lam):
    bsz, s, c = x.shape
    xb = x.reshape(bsz, s, LRU_BLOCKS, HEAD_DIM)
    r = jax.nn.sigmoid(jnp.einsum('bsnc,ncd->bsnd', xb, w_a).reshape(bsz, s, c) + b_a)
    i = jax.nn.sigmoid(jnp.einsum('bsnc,ncd->bsnd', xb, w_x).reshape(bsz, s, c) + b_x)
    log_a = -LRU_C * r.astype(jnp.float32) * jax.nn.softplus(-lam.astype(jnp.float32))
    a = jnp.exp(log_a)
    bterm = jnp.sqrt(-jnp.expm1(2.0 * log_a)) * (i * x).astype(jnp.float32)
    _, h = lax.associative_scan(_linear_recurrence_combine, (a, bterm), axis=1)
    return h.astype(x.dtype)


def alibi_slopes(n):
    return jnp.exp2(-8.0 * (jnp.arange(n, dtype=jnp.float32) + 1.0) / n)


def dilated_group_attention(q, k, v, window, dil, slopes):
    bsz, s, h, dh = q.shape
    span = dil * Q_BLOCK
    sp = -(-s // span) * span
    sub_len = sp // dil
    nb = sub_len // Q_BLOCK
    pad = ((0, 0), (0, sp - s), (0, 0), (0, 0))

    def to_blocks(t):
        t = jnp.pad(t, pad).reshape(bsz, sub_len, dil, h, dh).transpose(0, 2, 1, 3, 4)
        return t.reshape(bsz, dil, nb, Q_BLOCK, h, dh)

    def with_prev(t):
        prev = jnp.pad(t, ((0, 0), (0, 0), (1, 0), (0, 0), (0, 0), (0, 0)))[:, :, :-1]
        return jnp.concatenate([prev, t], axis=3)

    qb = to_blocks(q)
    kb = with_prev(to_blocks(k))
    vb = with_prev(to_blocks(v))
    sc = jnp.einsum('brnqhd,brnkhd->brnhqk', qb, kb).astype(jnp.float32) * (dh ** -0.5)

    qi = jnp.arange(Q_BLOCK)[:, None]
    ki = jnp.arange(2 * Q_BLOCK)[None, :]
    rel = qi + Q_BLOCK - ki
    key_pos = jnp.arange(nb)[:, None, None] * Q_BLOCK - Q_BLOCK + ki[None]
    n_keys = window // dil
    valid = (rel >= 0)[None] & (rel <= n_keys)[None] & (key_pos >= 0)
    bias = -slopes[:, None, None] * (rel * dil).astype(jnp.float32)[None]
    sc = jnp.where(valid[:, None], sc + bias, NEG_INF)

    mx = jnp.max(sc, axis=-1, keepdims=True)
    lse = mx + jnp.log(jnp.sum(jnp.exp(sc - mx), axis=-1, keepdims=True))
    p = jnp.exp(sc - lse).astype(v.dtype)
    out = jnp.einsum('brnhqk,brnkhd->brnqhd', p, vb)
    out = out.reshape(bsz, dil, sub_len, h, dh).transpose(0, 2, 1, 3, 4).reshape(bsz, sp, h, dh)[:, :s]
    lse = lse[..., 0].transpose(0, 1, 2, 4, 3).reshape(bsz, dil, sub_len, h)
    lse = lse.transpose(0, 2, 1, 3).reshape(bsz, sp, h)[:, :s]
    return out, lse


def dilated_mixture_attention(q, k_shared, v_shared):
    slopes = alibi_slopes(DIL_HEADS)
    outs, lses = [], []
    for g, (window, dil) in enumerate(DIL_GROUPS):
        hs = slice(g * HEADS_PER_GROUP, (g + 1) * HEADS_PER_GROUP)
        o, l = dilated_group_attention(q[:, :, hs], k_shared[:, :, hs], v_shared[:, :, hs],
                                       window, dil, slopes[hs])
        outs.append(o)
        lses.append(l)
    w = jax.nn.softmax(jnp.stack(lses, axis=0), axis=0)
    comb = jnp.concatenate([outs[g] * w[g][..., None].astype(outs[g].dtype)
                            for g in range(len(DIL_GROUPS))], axis=2)
    bsz, s = q.shape[:2]
    return comb.reshape(bsz, s, DIL_WIDTH)


def recurrent_layer(h, mem_n, pre_mix_g, post_mix_g, pre_ffn_g, post_ffn_g, w_in, conv_w, conv_b,
                    gate_a_w, gate_a_b, gate_x_w, gate_x_b, lam, w_mem_kv, w_out, w_ffn_in, w_ffn_out):
    hn = rms_norm(h, pre_mix_g)
    proj = hn @ w_in
    xb = proj[..., :LRU_WIDTH]
    gb = proj[..., LRU_WIDTH:2 * LRU_WIDTH]
    qm = proj[..., 2 * LRU_WIDTH:]
    xc = causal_depthwise_conv(xb, conv_w, conv_b)
    y = rg_lru(xc, gate_a_w, gate_a_b, gate_x_w, gate_x_b, lam) * jax.nn.gelu(gb)
    m = memory_attention(qm, mem_n, w_mem_kv)
    mix = jnp.concatenate([y, m], axis=-1) @ w_out
    h = h + rms_norm(mix, post_mix_g)
    return swiglu_sublayer(h, pre_ffn_g, post_ffn_g, w_ffn_in, w_ffn_out)


def dilated_layer(h, mem_n, k_shared, v_shared, pre_mix_g, post_mix_g, pre_ffn_g, post_ffn_g,
                  w_in, w_mem_kv, w_out, w_ffn_in, w_ffn_out):
    bsz, s, _ = h.shape
    hn = rms_norm(h, pre_mix_g)
    proj = hn @ w_in
    q = proj[..., :DIL_WIDTH].reshape(bsz, s, DIL_HEADS, HEAD_DIM)
    a = dilated_mixture_attention(q, k_shared, v_shared)
    m = memory_attention(proj[..., DIL_WIDTH:], mem_n, w_mem_kv)
    mix = jnp.concatenate([a, m], axis=-1) @ w_out
    h = h + rms_norm(mix, post_mix_g)
    return swiglu_sublayer(h, pre_ffn_g, post_ffn_g, w_ffn_in, w_ffn_out)


def setup_inputs(seed: int = 0) -> dict:
    key = jax.random.key(seed)
    keys = iter(jax.random.split(key, 64))
    f32 = jnp.float32
    D = D_MODEL

    def nrm(shape, scale):
        return jax.random.normal(next(keys), shape, f32) * scale

    def gain(shape):
        return 1.0 + nrm(shape, 0.05)

    a0 = jax.random.uniform(next(keys), (N_A, LRU_WIDTH), f32, 0.9, 0.999)
    sig_l = a0 ** (1.0 / LRU_C)
    lam = jnp.log(sig_l) - jnp.log1p(-sig_l)

    return {
        "x": nrm((BATCH, SEQ, D), 1.0),
        "mem": nrm((BATCH, N_MEM, D), 1.0),
        "mem_norm_g": gain((D,)),
        "a_pre_mix_g": gain((N_A, D)),
        "a_post_mix_g": gain((N_A, D)),
        "a_pre_ffn_g": gain((N_A, D)),
        "a_post_ffn_g": gain((N_A, D)),
        "a_w_in": nrm((N_A, D, 2 * LRU_WIDTH + MEM_WIDTH), D ** -0.5),
        "a_conv_w": nrm((N_A, CONV_WIDTH, LRU_WIDTH), CONV_WIDTH ** -0.5),
        "a_conv_b": nrm((N_A, LRU_WIDTH), 0.02),
        "a_gate_a_w": nrm((N_A, LRU_BLOCKS, HEAD_DIM, HEAD_DIM), HEAD_DIM ** -0.5),
        "a_gate_a_b": nrm((N_A, LRU_WIDTH), 0.1),
        "a_gate_x_w": nrm((N_A, LRU_BLOCKS, HEAD_DIM, HEAD_DIM), HEAD_DIM ** -0.5),
        "a_gate_x_b": nrm((N_A, LRU_WIDTH), 0.1),
        "a_lambda": lam,
        "a_w_mem_kv": nrm((N_A, D, 2 * MEM_WIDTH), D ** -0.5),
        "a_w_out": nrm((N_A, LRU_WIDTH + MEM_WIDTH, D), (LRU_WIDTH + MEM_WIDTH) ** -0.5),
        "a_w_ffn_in": nrm((N_A, D, 2 * D_FF), D ** -0.5),
        "a_w_ffn_out": nrm((N_A, D_FF, D), D_FF ** -0.5),
        "kv_norm_g": gain((D,)),
        "w_kv_shared": nrm((D, 2 * DIL_WIDTH), D ** -0.5),
        "b_pre_mix_g": gain((N_B, D)),
        "b_post_mix_g": gain((N_B, D)),
        "b_pre_ffn_g": gain((N_B, D)),
        "b_post_ffn_g": gain((N_B, D)),
        "b_w_in": nrm((N_B, D, DIL_WIDTH + MEM_WIDTH), D ** -0.5),
        "b_w_mem_kv": nrm((N_B, D, 2 * MEM_WIDTH), D ** -0.5),
        "b_w_out": nrm((N_B, DIL_WIDTH + MEM_WIDTH, D), (DIL_WIDTH + MEM_WIDTH) ** -0.5),
        "b_w_ffn_in": nrm((N_B, D, 2 * D_FF), D ** -0.5),
        "b_w_ffn_out": nrm((N_B, D_FF, D), D_FF ** -0.5),
    }


def reference(x, mem, mem_norm_g, a_pre_mix_g, a_post_mix_g, a_pre_ffn_g, a_post_ffn_g, a_w_in,
              a_conv_w, a_conv_b, a_gate_a_w, a_gate_a_b, a_gate_x_w, a_gate_x_b, a_lambda,
              a_w_mem_kv, a_w_out, a_w_ffn_in, a_w_ffn_out, kv_norm_g, w_kv_shared,
              b_pre_mix_g, b_post_mix_g, b_pre_ffn_g, b_post_ffn_g, b_w_in, b_w_mem_kv, b_w_out,
              b_w_ffn_in, b_w_ffn_out):
    bsz, s, _ = x.shape
    mem_n = rms_norm(mem, mem_norm_g)
    h = x
    k_shared = v_shared = None
    for l in range(DEPTH):
        if l < N_A:
            h = recurrent_layer(h, mem_n, a_pre_mix_g[l], a_post_mix_g[l], a_pre_ffn_g[l],
                                a_post_ffn_g[l], a_w_in[l], a_conv_w[l], a_conv_b[l],
                                a_gate_a_w[l], a_gate_a_b[l], a_gate_x_w[l], a_gate_x_b[l],
                                a_lambda[l], a_w_mem_kv[l], a_w_out[l], a_w_ffn_in[l],
                                a_w_ffn_out[l])
            if l == N_A - 1:
                kv = rms_norm(h, kv_norm_g) @ w_kv_shared
                k_shared = kv[..., :DIL_WIDTH].reshape(bsz, s, DIL_HEADS, HEAD_DIM)
                v_shared = kv[..., DIL_WIDTH:].reshape(bsz, s, DIL_HEADS, HEAD_DIM)
        else:
            j = l - N_A
            h = dilated_layer(h, mem_n, k_shared, v_shared, b_pre_mix_g[j], b_post_mix_g[j],
                              b_pre_ffn_g[j], b_post_ffn_g[j], b_w_in[j], b_w_mem_kv[j],
                              b_w_out[j], b_w_ffn_in[j], b_w_ffn_out[j])
    return h
```

```python
import functools
import math

import jax
import jax.numpy as jnp
from jax import lax
from jax.experimental import pallas as pl
from jax.experimental.pallas import tpu as pltpu

HEAD_DIM = 128
MEM_HEADS = 4
MEM_WIDTH = MEM_HEADS * HEAD_DIM
CONV_WIDTH = 4
LRU_C = 8.0
DIL_GROUPS = ((128, 1), (512, 4), (2048, 16))
HEADS_PER_GROUP = 4
GROUP_WIDTH = HEADS_PER_GROUP * HEAD_DIM
Q_BLOCK = 128
RMS_EPS = 1e-6
NEG_INF = -1e30

SUBLANES = 8
VMEM_LIMIT_BYTES = 56 * 1024 * 1024

F32 = jnp.float32
BF16 = jnp.bfloat16


def _params(*semantics):
    return pltpu.CompilerParams(dimension_semantics=semantics, vmem_limit_bytes=VMEM_LIMIT_BYTES)


def _rms(x, g):
    return x * lax.rsqrt(jnp.mean(x * x, axis=-1, keepdims=True) + RMS_EPS) * g


def _row_tile(rows, target):
    t = min(rows, target)
    assert rows % t == 0, (rows, t)
    return t


def _norm_matmul_kernel(h_ref, g_ref, w_ref, o_ref, hn_ref):
    @pl.when(pl.program_id(1) == 0)
    def _():
        hn_ref[...] = _rms(h_ref[...], g_ref[...]).astype(hn_ref.dtype)

    o_ref[...] = jnp.dot(hn_ref[...], w_ref[...], preferred_element_type=F32).astype(o_ref.dtype)


def _norm_matmul(h, g, w, *, tm=1024, tn=512):
    m, d = h.shape
    n = w.shape[1]
    tm = _row_tile(m, tm)
    tn = _row_tile(n, tn)
    return pl.pallas_call(
        _norm_matmul_kernel,
        grid=(m // tm, n // tn),
        in_specs=[
            pl.BlockSpec((tm, d), lambda i, j: (i, 0)),
            pl.BlockSpec((1, d), lambda i, j: (0, 0)),
            pl.BlockSpec((d, tn), lambda i, j: (0, j)),
        ],
        out_specs=pl.BlockSpec((tm, tn), lambda i, j: (i, j)),
        out_shape=jax.ShapeDtypeStruct((m, n), BF16),
        scratch_shapes=[pltpu.VMEM((tm, d), BF16)],
        compiler_params=_params("arbitrary", "arbitrary"),
        name="norm_matmul",
    )(h, g.reshape(1, d), w)


def _ffn_kernel(h_ref, pre_g_ref, post_g_ref, wg_ref, wu_ref, wo_ref, o_ref, hn_ref, acc_ref):
    j = pl.program_id(1)

    @pl.when(j == 0)
    def _():
        hn_ref[...] = _rms(h_ref[...], pre_g_ref[...]).astype(hn_ref.dtype)

    hn = hn_ref[...]
    gate = jnp.dot(hn, wg_ref[...], preferred_element_type=F32)
    up = jnp.dot(hn, wu_ref[...], preferred_element_type=F32)
    act = (gate * jax.nn.sigmoid(gate) * up).astype(BF16)
    part = jnp.dot(act, wo_ref[...], preferred_element_type=F32)

    @pl.when(j == 0)
    def _():
        acc_ref[...] = part

    @pl.when(j > 0)
    def _():
        acc_ref[...] += part

    @pl.when(j == pl.num_programs(1) - 1)
    def _():
        o_ref[...] = h_ref[...] + _rms(acc_ref[...], post_g_ref[...])


def _ffn(h, pre_g, post_g, w_in, w_out, *, tm=512, tf=512):
    m, d = h.shape
    f = w_out.shape[0]
    tm = _row_tile(m, tm)
    tf = _row_tile(f, tf)
    nf = f // tf
    return pl.pallas_call(
        _ffn_kernel,
        grid=(m // tm, nf),
        in_specs=[
            pl.BlockSpec((tm, d), lambda i, j: (i, 0)),
            pl.BlockSpec((1, d), lambda i, j: (0, 0)),
            pl.BlockSpec((1, d), lambda i, j: (0, 0)),
            pl.BlockSpec((d, tf), lambda i, j: (0, j)),
            pl.BlockSpec((d, tf), lambda i, j: (0, j + nf)),
            pl.BlockSpec((tf, d), lambda i, j: (j, 0)),
        ],
        out_specs=pl.BlockSpec((tm, d), lambda i, j: (i, 0)),
        out_shape=jax.ShapeDtypeStruct((m, d), F32),
        scratch_shapes=[pltpu.VMEM((tm, d), BF16), pltpu.VMEM((tm, d), F32)],
        compiler_params=_params("arbitrary", "arbitrary"),
        name="ffn",
    )(h, pre_g.reshape(1, d), post_g.reshape(1, d), w_in, w_in, w_out)


def _shift_rows(x, shift, fill):
    head = jnp.full((shift, x.shape[1]), fill, x.dtype)
    return jnp.concatenate([head, x[: x.shape[0] - shift]], axis=0)


def _rglru_kernel(xb_ref, gb_ref, cw_ref, cb_ref, wa_ref, ba_ref, wx_ref, bx_ref, lam_ref,
                  y_ref, xprev_ref, hprev_ref):
    @pl.when(pl.program_id(1) == 0)
    def _():
        xprev_ref[...] = jnp.zeros_like(xprev_ref)
        hprev_ref[...] = jnp.zeros_like(hprev_ref)

    ts = xb_ref.shape[0]
    row = lax.broadcasted_iota(jnp.int32, (ts, HEAD_DIM), 0)
    row8 = lax.broadcasted_iota(jnp.int32, (SUBLANES, HEAD_DIM), 0)
    for n in range(xb_ref.shape[1] // HEAD_DIM):
        sl = slice(n * HEAD_DIM, (n + 1) * HEAD_DIM)
        x = xb_ref[:, sl].astype(F32)
        xp = xprev_ref[:, sl]
        cw = cw_ref[:, sl]
        xc = cb_ref[:, sl] + cw[CONV_WIDTH - 1:CONV_WIDTH, :] * x
        for s in range(1, CONV_WIDTH):
            xs = pltpu.roll(x, s, 0)
            ps = pltpu.roll(xp, s, 0)
            first = jnp.where(row8 < s, ps, xs[:SUBLANES])
            xs = jnp.concatenate([first, xs[SUBLANES:]], axis=0)
            xc = xc + cw[CONV_WIDTH - 1 - s:CONV_WIDTH - s, :] * xs
        xprev_ref[:, sl] = x[ts - SUBLANES:, :]

        xcb = xc.astype(BF16)
        r = jax.nn.sigmoid(jnp.dot(xcb, wa_ref[n], preferred_element_type=F32) + ba_ref[:, sl])
        i = jax.nn.sigmoid(jnp.dot(xcb, wx_ref[n], preferred_element_type=F32) + bx_ref[:, sl])
        neg_lam = -lam_ref[:, sl]
        softplus = jnp.maximum(neg_lam, 0.0) + jnp.log1p(jnp.exp(-jnp.abs(neg_lam)))
        log_a = -LRU_C * r * softplus
        a = jnp.exp(log_a)
        b = jnp.sqrt(1.0 - a * a) * (i * xc)

        shift = 1
        while shift < ts:
            if shift < SUBLANES:
                a_s = jnp.where(row >= shift, pltpu.roll(a, shift, 0), 1.0)
                b_s = jnp.where(row >= shift, pltpu.roll(b, shift, 0), 0.0)
            else:
                a_s = _shift_rows(a, shift, 1.0)
                b_s = _shift_rows(b, shift, 0.0)
            b = a * b_s + b
            a = a * a_s
            shift *= 2
        h = a * hprev_ref[:, sl] + b
        hprev_ref[:, sl] = h[ts - 1:ts, :]

        gb = gb_ref[:, sl].astype(F32)
        cdf = 0.5 * (1.0 + jnp.tanh(math.sqrt(2.0 / math.pi) * (gb + 0.044715 * (gb * gb * gb))))
        y_ref[:, sl] = (h * (gb * cdf)).astype(y_ref.dtype)


def _rglru(proj, conv_w, conv_b, wa, ba, wx, bx, lam, *, ts=512):
    bsz, s, _ = proj.shape
    c = conv_w.shape[1]
    ts = _row_tile(s, ts)
    vec = lambda v: v.reshape(1, c)
    full2 = lambda shape: pl.BlockSpec(shape, lambda b, i: (0, 0))
    full3 = lambda shape: pl.BlockSpec(shape, lambda b, i: (0, 0, 0))
    return pl.pallas_call(
        _rglru_kernel,
        grid=(bsz, s // ts),
        in_specs=[
            pl.BlockSpec((None, ts, c), lambda b, i: (b, i, 0)),
            pl.BlockSpec((None, ts, c), lambda b, i: (b, i, 1)),
            full2((CONV_WIDTH, c)), full2((1, c)),
            full3(wa.shape), full2((1, c)),
            full3(wx.shape), full2((1, c)),
            full2((1, c)),
        ],
        out_specs=pl.BlockSpec((None, ts, c), lambda b, i: (b, i, 0)),
        out_shape=jax.ShapeDtypeStruct((bsz, s, c), BF16),
        scratch_shapes=[pltpu.VMEM((SUBLANES, c), F32), pltpu.VMEM((1, c), F32)],
        compiler_params=_params("arbitrary", "arbitrary"),
        name="rglru",
    )(proj, proj, conv_w, vec(conv_b), wa, vec(ba), wx, vec(bx), vec(lam))


def _dil_attn_kernel(q_ref, kp_ref, kc_ref, vp_ref, vc_ref, o_ref, l_ref, *, dil, group):
    n = pl.program_id(2)
    rows = q_ref.shape[0]
    qi = lax.broadcasted_iota(jnp.int32, (Q_BLOCK, 2 * Q_BLOCK), 0)
    ki = lax.broadcasted_iota(jnp.int32, (Q_BLOCK, 2 * Q_BLOCK), 1)
    rel = qi + Q_BLOCK - ki
    valid = (rel >= 0) & (rel <= Q_BLOCK)
    valid_first = valid & (ki >= jnp.where(n == 0, Q_BLOCK, 0))
    dist = (rel * dil).astype(F32)
    n_heads = len(DIL_GROUPS) * HEADS_PER_GROUP
    for hh in range(HEADS_PER_GROUP):
        hs = slice(hh * HEAD_DIM, (hh + 1) * HEAD_DIM)
        slope = 2.0 ** (-8.0 * (group * HEADS_PER_GROUP + hh + 1) / n_heads)
        bias = -slope * dist
        for j in range(rows // Q_BLOCK):
            cur = slice(j * Q_BLOCK, (j + 1) * Q_BLOCK)
            if j == 0:
                k_prev, v_prev, mask = kp_ref[:, hs], vp_ref[:, hs], valid_first
            else:
                prev = slice((j - 1) * Q_BLOCK, j * Q_BLOCK)
                k_prev, v_prev, mask = kc_ref[prev, hs], vc_ref[prev, hs], valid
            k = jnp.concatenate([k_prev, kc_ref[cur, hs]], axis=0)
            v = jnp.concatenate([v_prev, vc_ref[cur, hs]], axis=0)
            sc = lax.dot_general(q_ref[cur, hs], k, (((1,), (1,)), ((), ())),
                                 preferred_element_type=F32) * (HEAD_DIM ** -0.5)
            sc = jnp.where(mask, sc + bias, NEG_INF)
            mx = jnp.max(sc, axis=-1, keepdims=True)
            e = jnp.exp(sc - mx)
            den = jnp.sum(e, axis=-1, keepdims=True)
            out = jnp.dot(e.astype(BF16), v, preferred_element_type=F32) / den
            o_ref[cur, hs] = out.astype(o_ref.dtype)
            l_ref[cur, hs] = jnp.broadcast_to(mx + jnp.log(den), (Q_BLOCK, HEAD_DIM))


def _dil_attn(proj, kv, group, *, rows=512):
    _, dil = DIL_GROUPS[group]
    bsz, s, d = proj.shape
    w2 = kv.shape[2]
    sub = s // dil
    assert s % (dil * Q_BLOCK) == 0
    rows = _row_tile(sub, rows)
    qb = d // GROUP_WIDTH
    kb = w2 // GROUP_WIDTH
    rpq = rows // Q_BLOCK
    qv = proj.reshape(bsz, sub, dil * d)
    kvv = kv.reshape(bsz, sub, dil * w2)
    cur = lambda off: pl.BlockSpec((None, rows, GROUP_WIDTH),
                                   lambda b, r, n: (b, n, r * kb + off))
    prev = lambda off: pl.BlockSpec((None, Q_BLOCK, GROUP_WIDTH),
                                    lambda b, r, n: (b, jnp.maximum(n * rpq - 1, 0), r * kb + off))
    out_spec = pl.BlockSpec((None, rows, GROUP_WIDTH), lambda b, r, n: (b, n, r))
    o, lse = pl.pallas_call(
        functools.partial(_dil_attn_kernel, dil=dil, group=group),
        grid=(bsz, dil, sub // rows),
        in_specs=[
            pl.BlockSpec((None, rows, GROUP_WIDTH), lambda b, r, n: (b, n, r * qb + group)),
            prev(group), cur(group), prev(kb // 2 + group), cur(kb // 2 + group),
        ],
        out_specs=[out_spec, out_spec],
        out_shape=[jax.ShapeDtypeStruct((bsz, sub, dil * GROUP_WIDTH), BF16),
                   jax.ShapeDtypeStruct((bsz, sub, dil * GROUP_WIDTH), F32)],
        compiler_params=_params("arbitrary", "arbitrary", "arbitrary"),
        name=f"dil_attn_g{group}",
    )(qv, kvv, kvv, kvv, kvv)
    return o.reshape(bsz, s, GROUP_WIDTH), lse.reshape(bsz, s, GROUP_WIDTH)


def _mem_attn(q, kv_ref):
    outs = []
    for hh in range(MEM_HEADS):
        hs = slice(hh * HEAD_DIM, (hh + 1) * HEAD_DIM)
        vs = slice(MEM_WIDTH + hh * HEAD_DIM, MEM_WIDTH + (hh + 1) * HEAD_DIM)
        sc = lax.dot_general(q[:, hs], kv_ref[:, hs], (((1,), (1,)), ((), ())),
                             preferred_element_type=F32) * (HEAD_DIM ** -0.5)
        e = jnp.exp(sc - jnp.max(sc, axis=-1, keepdims=True))
        den = jnp.sum(e, axis=-1, keepdims=True)
        outs.append(jnp.dot(e.astype(BF16), kv_ref[:, vs], preferred_element_type=F32) / den)
    return jnp.concatenate(outs, axis=-1)


def _mix_out_tail(mixer, q_ref, kv_ref, w_ref, g_ref, h_ref, o_ref):
    m = _mem_attn(q_ref[...], kv_ref)
    mix = jnp.concatenate([mixer, m.astype(BF16)], axis=-1)
    out = jnp.dot(mix, w_ref[...], preferred_element_type=F32)
    o_ref[...] = h_ref[...] + _rms(out, g_ref[...])


def _mix_out_rec_kernel(y_ref, q_ref, kv_ref, w_ref, g_ref, h_ref, o_ref):
    _mix_out_tail(y_ref[...], q_ref, kv_ref, w_ref, g_ref, h_ref, o_ref)


def _mix_out_dil_kernel(o0_ref, o1_ref, o2_ref, l0_ref, l1_ref, l2_ref,
                        q_ref, kv_ref, w_ref, g_ref, h_ref, o_ref):
    lses = [l0_ref[...], l1_ref[...], l2_ref[...]]
    mx = jnp.maximum(jnp.maximum(lses[0], lses[1]), lses[2])
    es = [jnp.exp(l - mx) for l in lses]
    den = es[0] + es[1] + es[2]
    parts = [(o_ref_g[...].astype(F32) * (e / den)).astype(BF16)
             for o_ref_g, e in zip((o0_ref, o1_ref, o2_ref), es)]
    _mix_out_tail(jnp.concatenate(parts, axis=-1), q_ref, kv_ref, w_ref, g_ref, h_ref, o_ref)


def _mix_out(mixer_inputs, proj, memkv, w_out, post_g, h, *, tm=512):
    bsz, s, d = h.shape
    tm = _row_tile(s, tm)
    q_block = proj.shape[2] // MEM_WIDTH - 1
    row_spec = lambda width, col=0: pl.BlockSpec((None, tm, width), lambda b, i: (b, i, col))
    kernel = _mix_out_rec_kernel if len(mixer_inputs) == 1 else _mix_out_dil_kernel
    return pl.pallas_call(
        kernel,
        grid=(bsz, s // tm),
        in_specs=[row_spec(a.shape[2]) for a in mixer_inputs] + [
            row_spec(MEM_WIDTH, q_block),
            pl.BlockSpec((None,) + memkv.shape[1:], lambda b, i: (b, 0, 0)),
            pl.BlockSpec(w_out.shape, lambda b, i: (0, 0)),
            pl.BlockSpec((1, d), lambda b, i: (0, 0)),
            row_spec(d),
        ],
        out_specs=row_spec(d),
        out_shape=jax.ShapeDtypeStruct((bsz, s, d), F32),
        compiler_params=_params("arbitrary", "arbitrary"),
        name="mix_out_rec" if len(mixer_inputs) == 1 else "mix_out_dil",
    )(*mixer_inputs, proj, memkv, w_out, post_g.reshape(1, d), h)


def kernel(x, mem, mem_norm_g, a_pre_mix_g, a_post_mix_g, a_pre_ffn_g, a_post_ffn_g, a_w_in, a_conv_w, a_conv_b, a_gate_a_w, a_gate_a_b, a_gate_x_w, a_gate_x_b, a_lambda, a_w_mem_kv, a_w_out, a_w_ffn_in, a_w_ffn_out, kv_norm_g, w_kv_shared, b_pre_mix_g, b_post_mix_g, b_pre_ffn_g, b_post_ffn_g, b_w_in, b_w_mem_kv, b_w_out, b_w_ffn_in, b_w_ffn_out):
    bsz, s, d = x.shape
    n_mem = mem.shape[1]
    n_a = a_w_in.shape[0]
    n_b = b_w_in.shape[0]
    bf = lambda w: w.astype(BF16)

    def flat(t):
        return t.reshape(bsz * s, t.shape[-1])

    def mem_kv(w):
        return _norm_matmul(mem.reshape(bsz * n_mem, d), mem_norm_g, bf(w)).reshape(bsz, n_mem, -1)

    def ffn(h, pre_g, post_g, w_in, w_out):
        return _ffn(flat(h), pre_g, post_g, bf(w_in), bf(w_out)).reshape(bsz, s, d)

    h = x
    for l in range(n_a):
        proj = _norm_matmul(flat(h), a_pre_mix_g[l], bf(a_w_in[l])).reshape(bsz, s, -1)
        y = _rglru(proj, a_conv_w[l], a_conv_b[l], bf(a_gate_a_w[l]), a_gate_a_b[l],
                   bf(a_gate_x_w[l]), a_gate_x_b[l], a_lambda[l])
        h = _mix_out([y], proj, mem_kv(a_w_mem_kv[l]), bf(a_w_out[l]), a_post_mix_g[l], h)
        h = ffn(h, a_pre_ffn_g[l], a_post_ffn_g[l], a_w_ffn_in[l], a_w_ffn_out[l])

    kv = _norm_matmul(flat(h), kv_norm_g, bf(w_kv_shared)).reshape(bsz, s, -1)
    for l in range(n_b):
        proj = _norm_matmul(flat(h), b_pre_mix_g[l], bf(b_w_in[l])).reshape(bsz, s, -1)
        outs, lses = zip(*[_dil_attn(proj, kv, g) for g in range(len(DIL_GROUPS))])
        h = _mix_out(list(outs) + list(lses), proj, mem_kv(b_w_mem_kv[l]), bf(b_w_out[l]),
                     b_post_mix_g[l], h)
        h = ffn(h, b_pre_ffn_g[l], b_post_ffn_g[l], b_w_ffn_in[l], b_w_ffn_out[l])
    return h
```

```python
import functools
import math

import jax
import jax.numpy as jnp
from jax import lax
from jax.experimental import pallas as pl
from jax.experimental.pallas import tpu as pltpu

HEAD_DIM = 128
MEM_HEADS = 4
MEM_WIDTH = MEM_HEADS * HEAD_DIM
CONV_WIDTH = 4
LRU_C = 8.0
DIL_GROUPS = ((128, 1), (512, 4), (2048, 16))
HEADS_PER_GROUP = 4
GROUP_WIDTH = HEADS_PER_GROUP * HEAD_DIM
Q_BLOCK = 128
RMS_EPS = 1e-6
NEG_INF = -1e30

SUBLANES = 8
VMEM_LIMIT_BYTES = 56 * 1024 * 1024

F32 = jnp.float32
BF16 = jnp.bfloat16


def _params(*semantics):
    return pltpu.CompilerParams(dimension_semantics=semantics, vmem_limit_bytes=VMEM_LIMIT_BYTES)


def _rms(x, g):
    return x * lax.rsqrt(jnp.mean(x * x, axis=-1, keepdims=True) + RMS_EPS) * g


def _row_tile(rows, target):
    t = min(rows, target)
    assert rows % t == 0, (rows, t)
    return t


def _norm_matmul_kernel(h_ref, g_ref, w_ref, o_ref, *, tn):
    hn = _rms(h_ref[...], g_ref[...]).astype(BF16)
    for c in range(0, o_ref.shape[1], tn):
        o_ref[:, c:c + tn] = jnp.dot(hn, w_ref[:, c:c + tn],
                                     preferred_element_type=F32).astype(o_ref.dtype)


def _resident(block_shape, index_map):
    return pl.BlockSpec(block_shape, index_map, pipeline_mode=pl.Buffered(1))


def _norm_matmul(h, g, w, layer, *, tm=512, tn=512):
    m, d = h.shape
    n = w.shape[2]
    tm = _row_tile(m, tm)
    tn = _row_tile(n, tn)
    return pl.pallas_call(
        functools.partial(_norm_matmul_kernel, tn=tn),
        grid=(m // tm,),
        in_specs=[
            pl.BlockSpec((tm, d), lambda i: (i, 0)),
            _resident((1, d), lambda i: (0, 0)),
            _resident((None, d, n), lambda i: (layer, 0, 0)),
        ],
        out_specs=pl.BlockSpec((tm, n), lambda i: (i, 0)),
        out_shape=jax.ShapeDtypeStruct((m, n), BF16),
        compiler_params=_params("arbitrary"),
        name="norm_matmul",
    )(h, g.reshape(1, d), w)


def _ffn_kernel(h_ref, pre_g_ref, post_g_ref, wg_ref, wu_ref, wo_ref, o_ref, hn_ref, acc_ref):
    j = pl.program_id(1)

    @pl.when(j == 0)
    def _():
        hn_ref[...] = _rms(h_ref[...], pre_g_ref[...]).astype(hn_ref.dtype)
        acc_ref[...] = jnp.zeros_like(acc_ref)

    hn = hn_ref[...]
    gate = jnp.dot(hn, wg_ref[...], preferred_element_type=F32)
    up = jnp.dot(hn, wu_ref[...], preferred_element_type=F32)
    act = (gate * jax.nn.sigmoid(gate) * up).astype(BF16)
    acc_ref[...] += jnp.dot(act, wo_ref[...], preferred_element_type=F32)

    @pl.when(j == pl.num_programs(1) - 1)
    def _():
        o_ref[...] = h_ref[...] + _rms(acc_ref[...], post_g_ref[...])


def _ffn(h, pre_g, post_g, w_in, w_out, layer, *, tm=512, tf=512):
    m, d = h.shape
    f = w_out.shape[1]
    tm = _row_tile(m, tm)
    tf = _row_tile(f, tf)
    nf = f // tf
    return pl.pallas_call(
        _ffn_kernel,
        grid=(m // tm, nf),
        in_specs=[
            pl.BlockSpec((tm, d), lambda i, j: (i, 0)),
            pl.BlockSpec((1, d), lambda i, j: (0, 0)),
            pl.BlockSpec((1, d), lambda i, j: (0, 0)),
            pl.BlockSpec((None, d, tf), lambda i, j: (layer, 0, j)),
            pl.BlockSpec((None, d, tf), lambda i, j: (layer, 0, j + nf)),
            pl.BlockSpec((None, tf, d), lambda i, j: (layer, j, 0)),
        ],
        out_specs=pl.BlockSpec((tm, d), lambda i, j: (i, 0)),
        out_shape=jax.ShapeDtypeStruct((m, d), F32),
        scratch_shapes=[pltpu.VMEM((tm, d), BF16), pltpu.VMEM((tm, d), F32)],
        compiler_params=_params("arbitrary", "arbitrary"),
        name="ffn",
    )(h, pre_g.reshape(1, d), post_g.reshape(1, d), w_in, w_in, w_out)


def _shift_rows(x, shift, fill):
    head = jnp.full((shift, x.shape[1]), fill, x.dtype)
    return jnp.concatenate([head, x[: x.shape[0] - shift]], axis=0)


def _rglru_kernel(xb_ref, gb_ref, cw_ref, cb_ref, wa_ref, ba_ref, wx_ref, bx_ref, lam_ref,
                  y_ref, xprev_ref, hprev_ref):
    @pl.when(pl.program_id(1) == 0)
    def _():
        xprev_ref[...] = jnp.zeros_like(xprev_ref)
        hprev_ref[...] = jnp.zeros_like(hprev_ref)

    ts = xb_ref.shape[0]
    row = lax.broadcasted_iota(jnp.int32, (ts, HEAD_DIM), 0)
    row8 = lax.broadcasted_iota(jnp.int32, (SUBLANES, HEAD_DIM), 0)
    for n in range(xb_ref.shape[1] // HEAD_DIM):
        sl = slice(n * HEAD_DIM, (n + 1) * HEAD_DIM)
        x = xb_ref[:, sl].astype(F32)
        xp = xprev_ref[:, sl]
        cw = cw_ref[:, sl]
        xc = cb_ref[:, sl] + cw[CONV_WIDTH - 1:CONV_WIDTH, :] * x
        for s in range(1, CONV_WIDTH):
            xs = pltpu.roll(x, s, 0)
            ps = pltpu.roll(xp, s, 0)
            first = jnp.where(row8 < s, ps, xs[:SUBLANES])
            xs = jnp.concatenate([first, xs[SUBLANES:]], axis=0)
            xc = xc + cw[CONV_WIDTH - 1 - s:CONV_WIDTH - s, :] * xs
        xprev_ref[:, sl] = x[ts - SUBLANES:, :]

        xcb = xc.astype(BF16)
        r = jax.nn.sigmoid(jnp.dot(xcb, wa_ref[n], preferred_element_type=F32) + ba_ref[:, sl])
        i = jax.nn.sigmoid(jnp.dot(xcb, wx_ref[n], preferred_element_type=F32) + bx_ref[:, sl])
        neg_lam = -lam_ref[:, sl]
        softplus = jnp.maximum(neg_lam, 0.0) + jnp.log1p(jnp.exp(-jnp.abs(neg_lam)))
        log_a = -LRU_C * r * softplus
        a = jnp.exp(log_a)
        b = jnp.sqrt(1.0 - a * a) * (i * xc)

        shift = 1
        while shift < ts:
            if shift < SUBLANES:
                a_s = jnp.where(row >= shift, pltpu.roll(a, shift, 0), 1.0)
                b_s = jnp.where(row >= shift, pltpu.roll(b, shift, 0), 0.0)
            else:
                a_s = _shift_rows(a, shift, 1.0)
                b_s = _shift_rows(b, shift, 0.0)
            b = a * b_s + b
            a = a * a_s
            shift *= 2
        h = a * hprev_ref[:, sl] + b
        hprev_ref[:, sl] = h[ts - 1:ts, :]

        gb = gb_ref[:, sl].astype(F32)
        cdf = 0.5 * (1.0 + jnp.tanh(math.sqrt(2.0 / math.pi) * (gb + 0.044715 * (gb * gb * gb))))
        y_ref[:, sl] = (h * (gb * cdf)).astype(y_ref.dtype)


def _rglru(proj, conv_w, conv_b, wa, ba, wx, bx, lam, layer, *, ts=512):
    bsz, s, _ = proj.shape
    c = conv_w.shape[1]
    ts = _row_tile(s, ts)
    vec = lambda v: v.reshape(1, c)
    full2 = lambda shape: pl.BlockSpec(shape, lambda b, i: (0, 0))
    gate_w = lambda w: pl.BlockSpec((None,) + w.shape[1:], lambda b, i: (layer, 0, 0, 0))
    return pl.pallas_call(
        _rglru_kernel,
        grid=(bsz, s // ts),
        in_specs=[
            pl.BlockSpec((None, ts, c), lambda b, i: (b, i, 0)),
            pl.BlockSpec((None, ts, c), lambda b, i: (b, i, 1)),
            full2((CONV_WIDTH, c)), full2((1, c)),
            gate_w(wa), full2((1, c)),
            gate_w(wx), full2((1, c)),
            full2((1, c)),
        ],
        out_specs=pl.BlockSpec((None, ts, c), lambda b, i: (b, i, 0)),
        out_shape=jax.ShapeDtypeStruct((bsz, s, c), BF16),
        scratch_shapes=[pltpu.VMEM((SUBLANES, c), F32), pltpu.VMEM((1, c), F32)],
        compiler_params=_params("arbitrary", "arbitrary"),
        name="rglru",
    )(proj, proj, conv_w, vec(conv_b), wa, vec(ba), wx, vec(bx), vec(lam))


def _band_masks(first_block, dil):
    qi = lax.broadcasted_iota(jnp.int32, (Q_BLOCK, 2 * Q_BLOCK), 0)
    ki = lax.broadcasted_iota(jnp.int32, (Q_BLOCK, 2 * Q_BLOCK), 1)
    rel = qi + Q_BLOCK - ki
    valid = (rel >= 0) & (rel <= Q_BLOCK)
    valid_first = valid & (ki >= jnp.where(first_block, Q_BLOCK, 0))
    return valid, valid_first, (rel * dil).astype(F32)


def _alibi_slope(group, hh):
    n_heads = len(DIL_GROUPS) * HEADS_PER_GROUP
    return 2.0 ** (-8.0 * (group * HEADS_PER_GROUP + hh + 1) / n_heads)


def _attend(q, k, v, bias, mask):
    sc = lax.dot_general(q, k, (((1,), (1,)), ((), ())),
                         preferred_element_type=F32) * (HEAD_DIM ** -0.5)
    sc = jnp.where(mask, sc + bias, NEG_INF)
    mx = jnp.max(sc, axis=-1, keepdims=True)
    e = jnp.exp(sc - mx)
    den = jnp.sum(e, axis=-1, keepdims=True)
    out = jnp.dot(e.astype(BF16), v, preferred_element_type=F32) / den
    return out, mx + jnp.log(den)


def _dil_attn_dense_kernel(q_ref, kp_ref, kc_ref, vp_ref, vc_ref, o_ref, l_ref, *, group):
    valid, valid_first, dist = _band_masks(pl.program_id(1) == 0, 1)
    for hh in range(HEADS_PER_GROUP):
        hs = slice(hh * HEAD_DIM, (hh + 1) * HEAD_DIM)
        bias = -_alibi_slope(group, hh) * dist
        for j in range(q_ref.shape[0] // Q_BLOCK):
            cur = slice(j * Q_BLOCK, (j + 1) * Q_BLOCK)
            if j == 0:
                k_prev, v_prev, mask = kp_ref[:, hs], vp_ref[:, hs], valid_first
            else:
                prev = slice((j - 1) * Q_BLOCK, j * Q_BLOCK)
                k_prev, v_prev, mask = kc_ref[prev, hs], vc_ref[prev, hs], valid
            k = jnp.concatenate([k_prev, kc_ref[cur, hs]], axis=0)
            v = jnp.concatenate([v_prev, vc_ref[cur, hs]], axis=0)
            out, lse = _attend(q_ref[cur, hs], k, v, bias, mask)
            o_ref[cur, hs] = out.astype(o_ref.dtype)
            l_ref[cur, hs] = jnp.broadcast_to(lse, (Q_BLOCK, HEAD_DIM))


def _dil_attn_strided_kernel(q_ref, kp_ref, kc_ref, vp_ref, vc_ref, o_ref, l_ref,
                             qs_ref, ks_ref, vs_ref, os_ref, ls_ref, *, dil, group):
    span = q_ref.shape[0]
    _, valid_first, dist = _band_masks(pl.program_id(1) == 0, dil)
    for hh in range(HEADS_PER_GROUP):
        hs = slice(hh * HEAD_DIM, (hh + 1) * HEAD_DIM)
        bias = -_alibi_slope(group, hh) * dist
        qs_ref[...] = q_ref[:, hs].astype(F32)
        ks_ref[:span] = kp_ref[:, hs].astype(F32)
        ks_ref[span:] = kc_ref[:, hs].astype(F32)
        vs_ref[:span] = vp_ref[:, hs].astype(F32)
        vs_ref[span:] = vc_ref[:, hs].astype(F32)

        for r in range(dil):
            q = qs_ref[pl.ds(r, Q_BLOCK, stride=dil), :].astype(BF16)
            k = ks_ref[pl.ds(r, 2 * Q_BLOCK, stride=dil), :].astype(BF16)
            v = vs_ref[pl.ds(r, 2 * Q_BLOCK, stride=dil), :].astype(BF16)
            out, lse = _attend(q, k, v, bias, valid_first)
            os_ref[pl.ds(r, Q_BLOCK, stride=dil), :] = out
            ls_ref[pl.ds(r, Q_BLOCK, stride=dil), :] = jnp.broadcast_to(lse, (Q_BLOCK, HEAD_DIM))
        o_ref[:, hs] = os_ref[...].astype(o_ref.dtype)
        l_ref[:, hs] = ls_ref[...]


def _dil_attn(proj, kv, group, *, dense_rows=512):
    _, dil = DIL_GROUPS[group]
    bsz, s, _ = proj.shape
    v_off = kv.shape[2] // (2 * GROUP_WIDTH)
    assert s % (dil * Q_BLOCK) == 0
    if dil == 1:
        rows, prev_rows = _row_tile(s, dense_rows), Q_BLOCK
        body = functools.partial(_dil_attn_dense_kernel, group=group)
        scratch = []
    else:
        rows = prev_rows = dil * Q_BLOCK
        body = functools.partial(_dil_attn_strided_kernel, dil=dil, group=group)
        stage = lambda n_rows: pltpu.VMEM((n_rows, HEAD_DIM), F32)
        scratch = [stage(rows), stage(2 * rows), stage(2 * rows), stage(rows), stage(rows)]
    ratio = rows // prev_rows
    cur = lambda col: pl.BlockSpec((None, rows, GROUP_WIDTH), lambda b, n: (b, n, col))
    prev = lambda col: pl.BlockSpec((None, prev_rows, GROUP_WIDTH),
                                    lambda b, n: (b, jnp.maximum(n * ratio - 1, 0), col))
    return pl.pallas_call(
        body,
        grid=(bsz, s // rows),
        in_specs=[cur(group), prev(group), cur(group), prev(v_off + group), cur(v_off + group)],
        out_specs=[cur(0), cur(0)],
        out_shape=[jax.ShapeDtypeStruct((bsz, s, GROUP_WIDTH), BF16),
                   jax.ShapeDtypeStruct((bsz, s, GROUP_WIDTH), F32)],
        scratch_shapes=scratch,
        compiler_params=_params("arbitrary", "arbitrary"),
        name=f"dil_attn_g{group}",
    )(proj, kv, kv, kv, kv)


def _mem_attn(q, kv_ref):
    outs = []
    for hh in range(MEM_HEADS):
        hs = slice(hh * HEAD_DIM, (hh + 1) * HEAD_DIM)
        vs = slice(MEM_WIDTH + hh * HEAD_DIM, MEM_WIDTH + (hh + 1) * HEAD_DIM)
        sc = lax.dot_general(q[:, hs], kv_ref[:, hs], (((1,), (1,)), ((), ())),
                             preferred_element_type=F32) * (HEAD_DIM ** -0.5)
        e = jnp.exp(sc - jnp.max(sc, axis=-1, keepdims=True))
        den = jnp.sum(e, axis=-1, keepdims=True)
        outs.append(jnp.dot(e.astype(BF16), kv_ref[:, vs], preferred_element_type=F32) / den)
    return jnp.concatenate(outs, axis=-1)


def _mix_out_tail(mixer, q_ref, kv_ref, w_ref, g_ref, h_ref, o_ref):
    m = _mem_attn(q_ref[...], kv_ref)
    mix = jnp.concatenate([mixer, m.astype(BF16)], axis=-1)
    out = jnp.dot(mix, w_ref[...], preferred_element_type=F32)
    o_ref[...] = h_ref[...] + _rms(out, g_ref[...])


def _mix_out_rec_kernel(y_ref, q_ref, kv_ref, w_ref, g_ref, h_ref, o_ref):
    _mix_out_tail(y_ref[...], q_ref, kv_ref, w_ref, g_ref, h_ref, o_ref)


def _mix_out_dil_kernel(o0_ref, o1_ref, o2_ref, l0_ref, l1_ref, l2_ref,
                        q_ref, kv_ref, w_ref, g_ref, h_ref, o_ref):
    lses = [l0_ref[...], l1_ref[...], l2_ref[...]]
    mx = jnp.maximum(jnp.maximum(lses[0], lses[1]), lses[2])
    es = [jnp.exp(l - mx) for l in lses]
    den = es[0] + es[1] + es[2]
    parts = [(o_ref_g[...].astype(F32) * (e / den)).astype(BF16)
             for o_ref_g, e in zip((o0_ref, o1_ref, o2_ref), es)]
    _mix_out_tail(jnp.concatenate(parts, axis=-1), q_ref, kv_ref, w_ref, g_ref, h_ref, o_ref)


def _mix_out(mixer_inputs, proj, memkv, w_out, post_g, h, layer, *, tm=512):
    bsz, s, d = h.shape
    tm = _row_tile(s, tm)
    q_block = proj.shape[2] // MEM_WIDTH - 1
    row_spec = lambda width, col=0: pl.BlockSpec((None, tm, width), lambda b, i: (b, i, col))
    kernel = _mix_out_rec_kernel if len(mixer_inputs) == 1 else _mix_out_dil_kernel
    return pl.pallas_call(
        kernel,
        grid=(bsz, s // tm),
        in_specs=[row_spec(a.shape[2]) for a in mixer_inputs] + [
            row_spec(MEM_WIDTH, q_block),
            pl.BlockSpec((None,) + memkv.shape[1:], lambda b, i: (b, 0, 0)),
            _resident((None,) + w_out.shape[1:], lambda b, i: (layer, 0, 0)),
            _resident((1, d), lambda b, i: (0, 0)),
            row_spec(d),
        ],
        out_specs=row_spec(d),
        out_shape=jax.ShapeDtypeStruct((bsz, s, d), F32),
        compiler_params=_params("arbitrary", "arbitrary"),
        name="mix_out_rec" if len(mixer_inputs) == 1 else "mix_out_dil",
    )(*mixer_inputs, proj, memkv, w_out, post_g.reshape(1, d), h)


def kernel(x, mem, mem_norm_g, a_pre_mix_g, a_post_mix_g, a_pre_ffn_g, a_post_ffn_g, a_w_in, a_conv_w, a_conv_b, a_gate_a_w, a_gate_a_b, a_gate_x_w, a_gate_x_b, a_lambda, a_w_mem_kv, a_w_out, a_w_ffn_in, a_w_ffn_out, kv_norm_g, w_kv_shared, b_pre_mix_g, b_post_mix_g, b_pre_ffn_g, b_post_ffn_g, b_w_in, b_w_mem_kv, b_w_out, b_w_ffn_in, b_w_ffn_out):
    bsz, s, d = x.shape
    n_mem = mem.shape[1]
    n_a = a_w_in.shape[0]
    n_b = b_w_in.shape[0]
    bf = lambda w: w.astype(BF16)
    a_w_in, a_gate_a_w, a_gate_x_w, a_w_mem_kv, a_w_out, a_w_ffn_in, a_w_ffn_out = map(
        bf, (a_w_in, a_gate_a_w, a_gate_x_w, a_w_mem_kv, a_w_out, a_w_ffn_in, a_w_ffn_out))
    b_w_in, b_w_mem_kv, b_w_out, b_w_ffn_in, b_w_ffn_out = map(
        bf, (b_w_in, b_w_mem_kv, b_w_out, b_w_ffn_in, b_w_ffn_out))
    w_kv_shared = bf(w_kv_shared)[None]

    def flat(t):
        return t.reshape(bsz * s, t.shape[-1])

    def mem_kv(w, l):
        kv_mem = _norm_matmul(mem.reshape(bsz * n_mem, d), mem_norm_g, w, l)
        return kv_mem.reshape(bsz, n_mem, -1)

    def ffn(h, pre_g, post_g, w_in, w_out, l):
        return _ffn(flat(h), pre_g, post_g, w_in, w_out, l).reshape(bsz, s, d)

    h = x
    for l in range(n_a):
        proj = _norm_matmul(flat(h), a_pre_mix_g[l], a_w_in, l).reshape(bsz, s, -1)
        y = _rglru(proj, a_conv_w[l], a_conv_b[l], a_gate_a_w, a_gate_a_b[l],
                   a_gate_x_w, a_gate_x_b[l], a_lambda[l], l)
        h = _mix_out([y], proj, mem_kv(a_w_mem_kv, l), a_w_out, a_post_mix_g[l], h, l)
        h = ffn(h, a_pre_ffn_g[l], a_post_ffn_g[l], a_w_ffn_in, a_w_ffn_out, l)

    kv = _norm_matmul(flat(h), kv_norm_g, w_kv_shared, 0).reshape(bsz, s, -1)
    for l in range(n_b):
        proj = _norm_matmul(flat(h), b_pre_mix_g[l], b_w_in, l).reshape(bsz, s, -1)
        outs, lses = zip(*[_dil_attn(proj, kv, g) for g in range(len(DIL_GROUPS))])
        h = _mix_out(list(outs) + list(lses), proj, mem_kv(b_w_mem_kv, l), b_w_out,
                     b_post_mix_g[l], h, l)
        h = ffn(h, b_pre_ffn_g[l], b_post_ffn_g[l], b_w_ffn_in, b_w_ffn_out, l)
    return h
```

```python
import functools
import math

import jax
import jax.numpy as jnp
from jax import lax
from jax.experimental import pallas as pl
from jax.experimental.pallas import tpu as pltpu

HEAD_DIM = 128
MEM_HEADS = 4
MEM_WIDTH = MEM_HEADS * HEAD_DIM
CONV_WIDTH = 4
LRU_C = 8.0
DIL_GROUPS = ((128, 1), (512, 4), (2048, 16))
HEADS_PER_GROUP = 4
GROUP_WIDTH = HEADS_PER_GROUP * HEAD_DIM
Q_BLOCK = 128
RMS_EPS = 1e-6
NEG_INF = -1e30

SUBLANES = 8
V7X_VMEM_BYTES = 64 * 1024 * 1024
VMEM_LIMIT_BYTES = V7X_VMEM_BYTES - 8 * 1024 * 1024
FFN_VMEM_LIMIT_BYTES = V7X_VMEM_BYTES - 2 * 1024 * 1024

F32 = jnp.float32
BF16 = jnp.bfloat16


def _params(*semantics, vmem_limit_bytes=VMEM_LIMIT_BYTES):
    return pltpu.CompilerParams(dimension_semantics=semantics, vmem_limit_bytes=vmem_limit_bytes)


def _rms(x, g):
    return x * lax.rsqrt(jnp.mean(x * x, axis=-1, keepdims=True) + RMS_EPS) * g


def _row_tile(rows, target):
    t = min(rows, target)
    assert rows % t == 0, (rows, t)
    return t


def _norm_matmul_kernel(h_ref, g_ref, w_ref, o_ref, *, tn):
    hn = _rms(h_ref[...], g_ref[...]).astype(BF16)
    for c in range(0, o_ref.shape[1], tn):
        o_ref[:, c:c + tn] = jnp.dot(hn, w_ref[:, c:c + tn],
                                     preferred_element_type=F32).astype(o_ref.dtype)


def _resident(block_shape, index_map):
    return pl.BlockSpec(block_shape, index_map, pipeline_mode=pl.Buffered(1))


def _norm_matmul(h, g, w, layer, *, tm=512, tn=512):
    m, d = h.shape
    n = w.shape[2]
    tm = _row_tile(m, tm)
    tn = _row_tile(n, tn)
    return pl.pallas_call(
        functools.partial(_norm_matmul_kernel, tn=tn),
        grid=(m // tm,),
        in_specs=[
            pl.BlockSpec((tm, d), lambda i: (i, 0)),
            _resident((1, d), lambda i: (0, 0)),
            _resident((None, d, n), lambda i: (layer, 0, 0)),
        ],
        out_specs=pl.BlockSpec((tm, n), lambda i: (i, 0)),
        out_shape=jax.ShapeDtypeStruct((m, n), BF16),
        compiler_params=_params("arbitrary"),
        name="norm_matmul",
    )(h, g.reshape(1, d), w)


def _ffn_kernel(h_ref, pre_g_ref, post_g_ref, wg_ref, wu_ref, wo_ref, o_ref, hn_ref):
    j = pl.program_id(1)

    @pl.when(j == 0)
    def _():
        hn_ref[...] = _rms(h_ref[...], pre_g_ref[...]).astype(hn_ref.dtype)
        o_ref[...] = jnp.zeros_like(o_ref)

    hn = hn_ref[...]
    gate = jnp.dot(hn, wg_ref[...], preferred_element_type=F32)
    up = jnp.dot(hn, wu_ref[...], preferred_element_type=F32)
    act = (gate * jax.nn.sigmoid(gate) * up).astype(BF16)
    o_ref[...] += jnp.dot(act, wo_ref[...], preferred_element_type=F32)

    @pl.when(j == pl.num_programs(1) - 1)
    def _():
        o_ref[...] = h_ref[...] + _rms(o_ref[...], post_g_ref[...])


def _ffn(h, pre_g, post_g, w_in, w_out, layer, *, tm=1024, tf=512):
    m, d = h.shape
    f = w_out.shape[1]
    tm = _row_tile(m, tm)
    tf = _row_tile(f, tf)
    nf = f // tf
    return pl.pallas_call(
        _ffn_kernel,
        grid=(m // tm, nf),
        in_specs=[
            pl.BlockSpec((tm, d), lambda i, j: (i, 0)),
            pl.BlockSpec((1, d), lambda i, j: (0, 0)),
            pl.BlockSpec((1, d), lambda i, j: (0, 0)),
            pl.BlockSpec((None, d, tf), lambda i, j: (layer, 0, j)),
            pl.BlockSpec((None, d, tf), lambda i, j: (layer, 0, j + nf)),
            pl.BlockSpec((None, tf, d), lambda i, j: (layer, j, 0)),
        ],
        out_specs=pl.BlockSpec((tm, d), lambda i, j: (i, 0)),
        out_shape=jax.ShapeDtypeStruct((m, d), F32),
        scratch_shapes=[pltpu.VMEM((tm, d), BF16)],
        compiler_params=_params("arbitrary", "arbitrary", vmem_limit_bytes=FFN_VMEM_LIMIT_BYTES),
        name="ffn",
    )(h, pre_g.reshape(1, d), post_g.reshape(1, d), w_in, w_in, w_out)


def _rglru_kernel(xb_ref, gb_ref, cw_ref, cb_ref, wa_ref, ba_ref, wx_ref, bx_ref, lam_ref,
                  y_ref, xs_ref, as_ref, bs_ref, hprev_ref):
    ts = xb_ref.shape[0]
    pad = SUBLANES
    n_blocks = xb_ref.shape[1] // HEAD_DIM

    def steps(first, count):
        return pl.ds(2 * (pad + first), count, stride=2)

    @pl.when(pl.program_id(1) == 0)
    def _():
        hprev_ref[...] = jnp.zeros_like(hprev_ref)
        for n in range(n_blocks):
            xs_ref[n, steps(ts - pad, pad), :] = jnp.zeros((pad, HEAD_DIM), F32)

    for n in range(n_blocks):
        xs_ref[n, steps(-pad, pad), :] = xs_ref[n, steps(ts - pad, pad), :]
        as_ref[n, steps(-pad, pad), :] = jnp.ones((pad, HEAD_DIM), F32)
        bs_ref[n, steps(-pad, pad), :] = jnp.zeros((pad, HEAD_DIM), F32)

    for n in range(n_blocks):
        sl = slice(n * HEAD_DIM, (n + 1) * HEAD_DIM)
        x = xb_ref[:, sl].astype(F32)
        xs_ref[n, steps(0, ts), :] = x
        cw = cw_ref[:, sl]
        xc = cb_ref[:, sl] + cw[CONV_WIDTH - 1:CONV_WIDTH, :] * x
        for s in range(1, CONV_WIDTH):
            xc = xc + cw[CONV_WIDTH - 1 - s:CONV_WIDTH - s, :] * xs_ref[n, steps(-s, ts), :]

        xcb = xc.astype(BF16)
        r = jax.nn.sigmoid(jnp.dot(xcb, wa_ref[n], preferred_element_type=F32) + ba_ref[:, sl])
        i = jax.nn.sigmoid(jnp.dot(xcb, wx_ref[n], preferred_element_type=F32) + bx_ref[:, sl])
        neg_lam = -lam_ref[:, sl]
        softplus = jnp.maximum(neg_lam, 0.0) + jnp.log1p(jnp.exp(-jnp.abs(neg_lam)))
        a = jnp.exp2(r * ((-LRU_C * math.log2(math.e)) * softplus))
        z = 1.0 - a * a
        b = jnp.where(z > 0.0, z * lax.rsqrt(z), 0.0) * (i * xc)

        for shift in (1, 2, 4):
            as_ref[n, steps(0, ts), :] = a
            bs_ref[n, steps(0, ts), :] = b
            b = a * bs_ref[n, steps(-shift, ts), :] + b
            a = a * as_ref[n, steps(-shift, ts), :]
        hg = a[:SUBLANES] * hprev_ref[:, sl] + b[:SUBLANES]
        groups = [hg]
        for v in range(1, ts // SUBLANES):
            rows = slice(v * SUBLANES, (v + 1) * SUBLANES)
            hg = a[rows] * hg + b[rows]
            groups.append(hg)
        h = jnp.concatenate(groups, axis=0)
        hprev_ref[:, sl] = hg[SUBLANES - 1:, :]

        gb = gb_ref[:, sl].astype(F32)
        k = math.sqrt(2.0 / math.pi)
        cdf = 0.5 * (1.0 + jnp.tanh(gb * (k + (k * 0.044715) * (gb * gb))))
        y_ref[:, sl] = (h * (gb * cdf)).astype(y_ref.dtype)


def _rglru(proj, conv_w, conv_b, wa, ba, wx, bx, lam, layer, *, ts=512):
    bsz, s, _ = proj.shape
    c = conv_w.shape[1]
    ts = _row_tile(s, ts)
    vec = lambda v: v.reshape(1, c)
    full2 = lambda shape: pl.BlockSpec(shape, lambda b, i: (0, 0))
    gate_w = lambda w: pl.BlockSpec((None,) + w.shape[1:], lambda b, i: (layer, 0, 0, 0))
    return pl.pallas_call(
        _rglru_kernel,
        grid=(bsz, s // ts),
        in_specs=[
            pl.BlockSpec((None, ts, c), lambda b, i: (b, i, 0)),
            pl.BlockSpec((None, ts, c), lambda b, i: (b, i, 1)),
            full2((CONV_WIDTH, c)), full2((1, c)),
            gate_w(wa), full2((1, c)),
            gate_w(wx), full2((1, c)),
            full2((1, c)),
        ],
        out_specs=pl.BlockSpec((None, ts, c), lambda b, i: (b, i, 0)),
        out_shape=jax.ShapeDtypeStruct((bsz, s, c), BF16),
        scratch_shapes=[pltpu.VMEM((c // HEAD_DIM, 2 * (ts + SUBLANES), HEAD_DIM), F32)] * 3
        + [pltpu.VMEM((1, c), F32)],
        compiler_params=_params("arbitrary", "arbitrary"),
        name="rglru",
    )(proj, proj, conv_w, vec(conv_b), wa, vec(ba), wx, vec(bx), vec(lam))


def _band_masks(first_block, dil):
    qi = lax.broadcasted_iota(jnp.int32, (Q_BLOCK, 2 * Q_BLOCK), 0)
    ki = lax.broadcasted_iota(jnp.int32, (Q_BLOCK, 2 * Q_BLOCK), 1)
    rel = qi + Q_BLOCK - ki
    valid = (rel >= 0) & (rel <= Q_BLOCK)
    valid_first = valid & (ki >= jnp.where(first_block, Q_BLOCK, 0))
    return valid, valid_first, (rel * dil).astype(F32)


def _alibi_slope(group, hh):
    n_heads = len(DIL_GROUPS) * HEADS_PER_GROUP
    return 2.0 ** (-8.0 * (group * HEADS_PER_GROUP + hh + 1) / n_heads)


def _attend(q, k, v, bias, mask):
    sc = lax.dot_general(q, k, (((1,), (1,)), ((), ())),
                         preferred_element_type=F32) * (HEAD_DIM ** -0.5)
    sc = jnp.where(mask, sc + bias, NEG_INF)
    mx = jnp.max(sc, axis=-1, keepdims=True)
    e = jnp.exp(sc - mx)
    den = jnp.sum(e, axis=-1, keepdims=True)
    out = jnp.dot(e.astype(BF16), v, preferred_element_type=F32) / den
    return out, mx + jnp.log(den)


def _dil_attn_dense_kernel(q_ref, kp_ref, kc_ref, vp_ref, vc_ref, o_ref, l_ref, *, group):
    valid, valid_first, dist = _band_masks(pl.program_id(1) == 0, 1)
    for hh in range(HEADS_PER_GROUP):
        hs = slice(hh * HEAD_DIM, (hh + 1) * HEAD_DIM)
        bias = -_alibi_slope(group, hh) * dist
        for j in range(q_ref.shape[0] // Q_BLOCK):
            cur = slice(j * Q_BLOCK, (j + 1) * Q_BLOCK)
            if j == 0:
                k_prev, v_prev, mask = kp_ref[:, hs], vp_ref[:, hs], valid_first
            else:
                prev = slice((j - 1) * Q_BLOCK, j * Q_BLOCK)
                k_prev, v_prev, mask = kc_ref[prev, hs], vc_ref[prev, hs], valid
            k = jnp.concatenate([k_prev, kc_ref[cur, hs]], axis=0)
            v = jnp.concatenate([v_prev, vc_ref[cur, hs]], axis=0)
            out, lse = _attend(q_ref[cur, hs], k, v, bias, mask)
            o_ref[cur, hs] = out.astype(o_ref.dtype)
            l_ref[cur, hs] = jnp.broadcast_to(lse, (Q_BLOCK, HEAD_DIM))


def _dil_attn_strided_kernel(q_ref, kp_ref, kc_ref, vp_ref, vc_ref, o_ref, l_ref,
                             qs_ref, ks_ref, vs_ref, os_ref, ls_ref, *, dil, group):
    span = q_ref.shape[0]
    _, valid_first, dist = _band_masks(pl.program_id(1) == 0, dil)
    for hh in range(HEADS_PER_GROUP):
        hs = slice(hh * HEAD_DIM, (hh + 1) * HEAD_DIM)
        bias = -_alibi_slope(group, hh) * dist
        qs_ref[...] = q_ref[:, hs].astype(F32)
        ks_ref[:span] = kp_ref[:, hs].astype(F32)
        ks_ref[span:] = kc_ref[:, hs].astype(F32)
        vs_ref[:span] = vp_ref[:, hs].astype(F32)
        vs_ref[span:] = vc_ref[:, hs].astype(F32)

        for r in range(dil):
            q = qs_ref[pl.ds(r, Q_BLOCK, stride=dil), :].astype(BF16)
            k = ks_ref[pl.ds(r, 2 * Q_BLOCK, stride=dil), :].astype(BF16)
            v = vs_ref[pl.ds(r, 2 * Q_BLOCK, stride=dil), :].astype(BF16)
            out, lse = _attend(q, k, v, bias, valid_first)
            os_ref[pl.ds(r, Q_BLOCK, stride=dil), :] = out
            ls_ref[pl.ds(r, Q_BLOCK, stride=dil), :] = jnp.broadcast_to(lse, (Q_BLOCK, HEAD_DIM))
        o_ref[:, hs] = os_ref[...].astype(o_ref.dtype)
        l_ref[:, hs] = ls_ref[...]


def _dil_attn(proj, kv, group, *, dense_rows=512):
    _, dil = DIL_GROUPS[group]
    bsz, s, _ = proj.shape
    v_off = kv.shape[2] // (2 * GROUP_WIDTH)
    assert s % (dil * Q_BLOCK) == 0
    if dil == 1:
        rows, prev_rows = _row_tile(s, dense_rows), Q_BLOCK
        body = functools.partial(_dil_attn_dense_kernel, group=group)
        scratch = []
    else:
        rows = prev_rows = dil * Q_BLOCK
        body = functools.partial(_dil_attn_strided_kernel, dil=dil, group=group)
        stage = lambda n_rows: pltpu.VMEM((n_rows, HEAD_DIM), F32)
        scratch = [stage(rows), stage(2 * rows), stage(2 * rows), stage(rows), stage(rows)]
    ratio = rows // prev_rows
    cur = lambda col: pl.BlockSpec((None, rows, GROUP_WIDTH), lambda b, n: (b, n, col))
    prev = lambda col: pl.BlockSpec((None, prev_rows, GROUP_WIDTH),
                                    lambda b, n: (b, jnp.maximum(n * ratio - 1, 0), col))
    return pl.pallas_call(
        body,
        grid=(bsz, s // rows),
        in_specs=[cur(group), prev(group), cur(group), prev(v_off + group), cur(v_off + group)],
        out_specs=[cur(0), cur(0)],
        out_shape=[jax.ShapeDtypeStruct((bsz, s, GROUP_WIDTH), BF16),
                   jax.ShapeDtypeStruct((bsz, s, GROUP_WIDTH), F32)],
        scratch_shapes=scratch,
        compiler_params=_params("arbitrary", "arbitrary"),
        name=f"dil_attn_g{group}",
    )(proj, kv, kv, kv, kv)


def _mem_attn(q, kv_ref):
    outs = []
    for hh in range(MEM_HEADS):
        hs = slice(hh * HEAD_DIM, (hh + 1) * HEAD_DIM)
        vs = slice(MEM_WIDTH + hh * HEAD_DIM, MEM_WIDTH + (hh + 1) * HEAD_DIM)
        sc = lax.dot_general(q[:, hs], kv_ref[:, hs], (((1,), (1,)), ((), ())),
                             preferred_element_type=F32) * (HEAD_DIM ** -0.5)
        e = jnp.exp(sc - jnp.max(sc, axis=-1, keepdims=True))
        den = jnp.sum(e, axis=-1, keepdims=True)
        outs.append(jnp.dot(e.astype(BF16), kv_ref[:, vs], preferred_element_type=F32) / den)
    return jnp.concatenate(outs, axis=-1)


def _mix_out_tail(mixer, q_ref, kv_ref, w_ref, g_ref, h_ref, o_ref):
    m = _mem_attn(q_ref[...], kv_ref)
    mix = jnp.concatenate([mixer, m.astype(BF16)], axis=-1)
    out = jnp.dot(mix, w_ref[...], preferred_element_type=F32)
    o_ref[...] = h_ref[...] + _rms(out, g_ref[...])


def _mix_out_rec_kernel(y_ref, q_ref, kv_ref, w_ref, g_ref, h_ref, o_ref):
    _mix_out_tail(y_ref[...], q_ref, kv_ref, w_ref, g_ref, h_ref, o_ref)


def _mix_out_dil_kernel(o0_ref, o1_ref, o2_ref, l0_ref, l1_ref, l2_ref,
                        q_ref, kv_ref, w_ref, g_ref, h_ref, o_ref):
    lses = [l0_ref[...], l1_ref[...], l2_ref[...]]
    mx = jnp.maximum(jnp.maximum(lses[0], lses[1]), lses[2])
    es = [jnp.exp(l - mx) for l in lses]
    den = es[0] + es[1] + es[2]
    parts = [(o_ref_g[...].astype(F32) * (e / den)).astype(BF16)
             for o_ref_g, e in zip((o0_ref, o1_ref, o2_ref), es)]
    _mix_out_tail(jnp.concatenate(parts, axis=-1), q_ref, kv_ref, w_ref, g_ref, h_ref, o_ref)


def _mix_out(mixer_inputs, proj, memkv, w_out, post_g, h, layer, *, tm=512):
    bsz, s, d = h.shape
    tm = _row_tile(s, tm)
    q_block = proj.shape[2] // MEM_WIDTH - 1
    row_spec = lambda width, col=0: pl.BlockSpec((None, tm, width), lambda b, i: (b, i, col))
    kernel = _mix_out_rec_kernel if len(mixer_inputs) == 1 else _mix_out_dil_kernel
    return pl.pallas_call(
        kernel,
        grid=(bsz, s // tm),
        in_specs=[row_spec(a.shape[2]) for a in mixer_inputs] + [
            row_spec(MEM_WIDTH, q_block),
            pl.BlockSpec((None,) + memkv.shape[1:], lambda b, i: (b, 0, 0)),
            _resident((None,) + w_out.shape[1:], lambda b, i: (layer, 0, 0)),
            _resident((1, d), lambda b, i: (0, 0)),
            row_spec(d),
        ],
        out_specs=row_spec(d),
        out_shape=jax.ShapeDtypeStruct((bsz, s, d), F32),
        compiler_params=_params("arbitrary", "arbitrary"),
        name="mix_out_rec" if len(mixer_inputs) == 1 else "mix_out_dil",
    )(*mixer_inputs, proj, memkv, w_out, post_g.reshape(1, d), h)


def kernel(x, mem, mem_norm_g, a_pre_mix_g, a_post_mix_g, a_pre_ffn_g, a_post_ffn_g, a_w_in, a_conv_w, a_conv_b, a_gate_a_w, a_gate_a_b, a_gate_x_w, a_gate_x_b, a_lambda, a_w_mem_kv, a_w_out, a_w_ffn_in, a_w_ffn_out, kv_norm_g, w_kv_shared, b_pre_mix_g, b_post_mix_g, b_pre_ffn_g, b_post_ffn_g, b_w_in, b_w_mem_kv, b_w_out, b_w_ffn_in, b_w_ffn_out):
    bsz, s, d = x.shape
    n_mem = mem.shape[1]
    n_a = a_w_in.shape[0]
    n_b = b_w_in.shape[0]
    bf = lambda w: w.astype(BF16)
    a_w_in, a_gate_a_w, a_gate_x_w, a_w_mem_kv, a_w_out, a_w_ffn_in, a_w_ffn_out = map(
        bf, (a_w_in, a_gate_a_w, a_gate_x_w, a_w_mem_kv, a_w_out, a_w_ffn_in, a_w_ffn_out))
    b_w_in, b_w_mem_kv, b_w_out, b_w_ffn_in, b_w_ffn_out = map(
        bf, (b_w_in, b_w_mem_kv, b_w_out, b_w_ffn_in, b_w_ffn_out))
    w_kv_shared = bf(w_kv_shared)[None]

    def flat(t):
        return t.reshape(bsz * s, t.shape[-1])

    def mem_kv(w, l):
        kv_mem = _norm_matmul(mem.reshape(bsz * n_mem, d), mem_norm_g, w, l)
        return kv_mem.reshape(bsz, n_mem, -1)

    def ffn(h, pre_g, post_g, w_in, w_out, l):
        return _ffn(flat(h), pre_g, post_g, w_in, w_out, l).reshape(bsz, s, d)

    h = x
    for l in range(n_a):
        proj = _norm_matmul(flat(h), a_pre_mix_g[l], a_w_in, l).reshape(bsz, s, -1)
        y = _rglru(proj, a_conv_w[l], a_conv_b[l], a_gate_a_w, a_gate_a_b[l],
                   a_gate_x_w, a_gate_x_b[l], a_lambda[l], l)
        h = _mix_out([y], proj, mem_kv(a_w_mem_kv, l), a_w_out, a_post_mix_g[l], h, l)
        h = ffn(h, a_pre_ffn_g[l], a_post_ffn_g[l], a_w_ffn_in, a_w_ffn_out, l)

    kv = _norm_matmul(flat(h), kv_norm_g, w_kv_shared, 0).reshape(bsz, s, -1)
    for l in range(n_b):
        proj = _norm_matmul(flat(h), b_pre_mix_g[l], b_w_in, l).reshape(bsz, s, -1)
        outs, lses = zip(*[_dil_attn(proj, kv, g) for g in range(len(DIL_GROUPS))])
        h = _mix_out(list(outs) + list(lses), proj, mem_kv(b_w_mem_kv, l), b_w_out,
                     b_post_mix_g[l], h, l)
        h = ffn(h, b_pre_ffn_g[l], b_post_ffn_g[l], b_w_ffn_in, b_w_ffn_out, l)
    return h
```

```python
import functools
import math

import jax
import jax.numpy as jnp
from jax import lax
from jax.experimental import pallas as pl
from jax.experimental.pallas import tpu as pltpu

HEAD_DIM = 128
MEM_HEADS = 4
MEM_WIDTH = MEM_HEADS * HEAD_DIM
CONV_WIDTH = 4
LRU_C = 8.0
DIL_GROUPS = ((128, 1), (512, 4), (2048, 16))
HEADS_PER_GROUP = 4
GROUP_WIDTH = HEADS_PER_GROUP * HEAD_DIM
Q_BLOCK = 128
RMS_EPS = 1e-6
NEG_INF = -1e30

SUBLANES = 8
NORM_CHUNK_ROWS = 2 * SUBLANES
V7X_VMEM_BYTES = 64 * 1024 * 1024
VMEM_LIMIT_BYTES = V7X_VMEM_BYTES - 8 * 1024 * 1024
FFN_VMEM_LIMIT_BYTES = V7X_VMEM_BYTES - 2 * 1024 * 1024

F32 = jnp.float32
BF16 = jnp.bfloat16


def _params(*semantics, vmem_limit_bytes=VMEM_LIMIT_BYTES):
    return pltpu.CompilerParams(dimension_semantics=semantics, vmem_limit_bytes=vmem_limit_bytes)


def _rms(x, g):
    return x * lax.rsqrt(jnp.mean(x * x, axis=-1, keepdims=True) + RMS_EPS) * g


def _row_tile(rows, target):
    t = min(rows, target)
    assert rows % t == 0, (rows, t)
    return t


def _norm_matmul_kernel(h_ref, g_ref, w_ref, o_ref, *, tn):
    hn = _rms(h_ref[...], g_ref[...]).astype(BF16)
    for c in range(0, o_ref.shape[1], tn):
        o_ref[:, c:c + tn] = jnp.dot(hn, w_ref[:, c:c + tn],
                                     preferred_element_type=F32).astype(o_ref.dtype)


def _resident(block_shape, index_map):
    return pl.BlockSpec(block_shape, index_map, pipeline_mode=pl.Buffered(1))


def _norm_matmul(h, g, w, layer, *, tm=512, tn=512):
    m, d = h.shape
    n = w.shape[2]
    tm = _row_tile(m, tm)
    tn = _row_tile(n, tn)
    return pl.pallas_call(
        functools.partial(_norm_matmul_kernel, tn=tn),
        grid=(m // tm,),
        in_specs=[
            pl.BlockSpec((tm, d), lambda i: (i, 0)),
            _resident((1, d), lambda i: (0, 0)),
            _resident((None, d, n), lambda i: (layer, 0, 0)),
        ],
        out_specs=pl.BlockSpec((tm, n), lambda i: (i, 0)),
        out_shape=jax.ShapeDtypeStruct((m, n), BF16),
        compiler_params=_params("arbitrary"),
        name="norm_matmul",
    )(h, g.reshape(1, d), w)


def _ffn_kernel(h_ref, pre_g_ref, post_g_ref, wg_ref, wu_ref, wo_ref, o_ref, hn_ref):
    j = pl.program_id(1)

    @pl.when(j == 0)
    def _():
        hn_ref[...] = _rms(h_ref[...], pre_g_ref[...]).astype(hn_ref.dtype)
        o_ref[...] = jnp.zeros_like(o_ref)

    hn = hn_ref[...]
    gate = jnp.dot(hn, wg_ref[...], preferred_element_type=F32)
    up = jnp.dot(hn, wu_ref[...], preferred_element_type=F32)
    act = (gate * jax.nn.sigmoid(gate) * up).astype(BF16)
    o_ref[...] += jnp.dot(act, wo_ref[...], preferred_element_type=F32)

    @pl.when(j == pl.num_programs(1) - 1)
    def _():
        for r in range(0, o_ref.shape[0], NORM_CHUNK_ROWS):
            rows = slice(r, r + NORM_CHUNK_ROWS)
            o_ref[rows, :] = h_ref[rows, :] + _rms(o_ref[rows, :], post_g_ref[...])


def _ffn(h, pre_g, post_g, w_in, w_out, layer, *, tm=1024, tf=512):
    m, d = h.shape
    f = w_out.shape[1]
    tm = _row_tile(m, tm)
    tf = _row_tile(f, tf)
    nf = f // tf
    return pl.pallas_call(
        _ffn_kernel,
        grid=(m // tm, nf),
        in_specs=[
            pl.BlockSpec((tm, d), lambda i, j: (i, 0)),
            pl.BlockSpec((1, d), lambda i, j: (0, 0)),
            pl.BlockSpec((1, d), lambda i, j: (0, 0)),
            pl.BlockSpec((None, d, tf), lambda i, j: (layer, 0, j)),
            pl.BlockSpec((None, d, tf), lambda i, j: (layer, 0, j + nf)),
            pl.BlockSpec((None, tf, d), lambda i, j: (layer, j, 0)),
        ],
        out_specs=pl.BlockSpec((tm, d), lambda i, j: (i, 0)),
        out_shape=jax.ShapeDtypeStruct((m, d), F32),
        scratch_shapes=[pltpu.VMEM((tm, d), BF16)],
        compiler_params=_params("arbitrary", "arbitrary", vmem_limit_bytes=FFN_VMEM_LIMIT_BYTES),
        name="ffn",
    )(h, pre_g.reshape(1, d), post_g.reshape(1, d), w_in, w_in, w_out)


LRU_PAD = SUBLANES


def _lru_steps(first, count):
    return pl.ds(2 * (LRU_PAD + first), count, stride=2)


def _rglru_reset(xs_ref, hprev_ref, ts):
    hprev_ref[...] = jnp.zeros_like(hprev_ref)
    for n in range(xs_ref.shape[0]):
        xs_ref[n, _lru_steps(ts - LRU_PAD, LRU_PAD), :] = jnp.zeros((LRU_PAD, HEAD_DIM), F32)


def _rglru_tile(x_of, gate_of, y_ref, cw_ref, cb_ref, wa_ref, ba_ref, wx_ref, bx_ref, lam_ref,
                xs_ref, as_ref, bs_ref, hprev_ref, ts):
    pad, steps = LRU_PAD, _lru_steps
    n_blocks = cw_ref.shape[1] // HEAD_DIM

    for n in range(n_blocks):
        xs_ref[n, steps(-pad, pad), :] = xs_ref[n, steps(ts - pad, pad), :]
        as_ref[n, steps(-pad, pad), :] = jnp.ones((pad, HEAD_DIM), F32)
        bs_ref[n, steps(-pad, pad), :] = jnp.zeros((pad, HEAD_DIM), F32)

    for n in range(n_blocks):
        sl = slice(n * HEAD_DIM, (n + 1) * HEAD_DIM)
        x = x_of(n).astype(F32)
        xs_ref[n, steps(0, ts), :] = x
        cw = cw_ref[:, sl]
        xc = cb_ref[:, sl] + cw[CONV_WIDTH - 1:CONV_WIDTH, :] * x
        for s in range(1, CONV_WIDTH):
            xc = xc + cw[CONV_WIDTH - 1 - s:CONV_WIDTH - s, :] * xs_ref[n, steps(-s, ts), :]

        xcb = xc.astype(BF16)
        r = jax.nn.sigmoid(jnp.dot(xcb, wa_ref[n], preferred_element_type=F32) + ba_ref[:, sl])
        i = jax.nn.sigmoid(jnp.dot(xcb, wx_ref[n], preferred_element_type=F32) + bx_ref[:, sl])
        neg_lam = -lam_ref[:, sl]
        softplus = jnp.maximum(neg_lam, 0.0) + jnp.log1p(jnp.exp(-jnp.abs(neg_lam)))
        a = jnp.exp2(r * ((-LRU_C * math.log2(math.e)) * softplus))
        z = 1.0 - a * a
        b = jnp.where(z > 0.0, z * lax.rsqrt(z), 0.0) * (i * xc)

        for shift in (1, 2, 4):
            as_ref[n, steps(0, ts), :] = a
            bs_ref[n, steps(0, ts), :] = b
            b = a * bs_ref[n, steps(-shift, ts), :] + b
            a = a * as_ref[n, steps(-shift, ts), :]
        hg = a[:SUBLANES] * hprev_ref[:, sl] + b[:SUBLANES]
        groups = [hg]
        for v in range(1, ts // SUBLANES):
            rows = slice(v * SUBLANES, (v + 1) * SUBLANES)
            hg = a[rows] * hg + b[rows]
            groups.append(hg)
        h = jnp.concatenate(groups, axis=0)
        hprev_ref[:, sl] = hg[SUBLANES - 1:, :]

        gb = gate_of(n).astype(F32)
        k = math.sqrt(2.0 / math.pi)
        cdf = 0.5 * (1.0 + jnp.tanh(gb * (k + (k * 0.044715) * (gb * gb))))
        y_ref[:, sl] = (h * (gb * cdf)).astype(y_ref.dtype)


def _rglru_kernel(xb_ref, gb_ref, cw_ref, cb_ref, wa_ref, ba_ref, wx_ref, bx_ref, lam_ref,
                  y_ref, xs_ref, as_ref, bs_ref, hprev_ref):
    ts = xb_ref.shape[0]

    @pl.when(pl.program_id(1) == 0)
    def _():
        _rglru_reset(xs_ref, hprev_ref, ts)

    _rglru_tile(lambda n: xb_ref[:, n * HEAD_DIM:(n + 1) * HEAD_DIM],
                lambda n: gb_ref[:, n * HEAD_DIM:(n + 1) * HEAD_DIM],
                y_ref, cw_ref, cb_ref, wa_ref, ba_ref, wx_ref, bx_ref, lam_ref,
                xs_ref, as_ref, bs_ref, hprev_ref, ts)


def _rglru(proj, conv_w, conv_b, wa, ba, wx, bx, lam, layer, *, ts=512):
    bsz, s, _ = proj.shape
    c = conv_w.shape[1]
    ts = _row_tile(s, ts)
    vec = lambda v: v.reshape(1, c)
    full2 = lambda shape: pl.BlockSpec(shape, lambda b, i: (0, 0))
    gate_w = lambda w: pl.BlockSpec((None,) + w.shape[1:], lambda b, i: (layer, 0, 0, 0))
    stage = pltpu.VMEM((c // HEAD_DIM, 2 * (ts + LRU_PAD), HEAD_DIM), F32)
    return pl.pallas_call(
        _rglru_kernel,
        grid=(bsz, s // ts),
        in_specs=[
            pl.BlockSpec((None, ts, c), lambda b, i: (b, i, 0)),
            pl.BlockSpec((None, ts, c), lambda b, i: (b, i, 1)),
            full2((CONV_WIDTH, c)), full2((1, c)),
            gate_w(wa), full2((1, c)),
            gate_w(wx), full2((1, c)),
            full2((1, c)),
        ],
        out_specs=pl.BlockSpec((None, ts, c), lambda b, i: (b, i, 0)),
        out_shape=jax.ShapeDtypeStruct((bsz, s, c), BF16),
        scratch_shapes=[stage, stage, stage, pltpu.VMEM((1, c), F32)],
        compiler_params=_params("arbitrary", "arbitrary"),
        name="rglru",
    )(proj, proj, conv_w, vec(conv_b), wa, vec(ba), wx, vec(bx), vec(lam))


def _band_masks(first_block, dil):
    qi = lax.broadcasted_iota(jnp.int32, (Q_BLOCK, 2 * Q_BLOCK), 0)
    ki = lax.broadcasted_iota(jnp.int32, (Q_BLOCK, 2 * Q_BLOCK), 1)
    rel = qi + Q_BLOCK - ki
    valid = (rel >= 0) & (rel <= Q_BLOCK)
    valid_first = valid & (ki >= jnp.where(first_block, Q_BLOCK, 0))
    return valid, valid_first, (rel * dil).astype(F32)


def _alibi_slope(group, hh):
    n_heads = len(DIL_GROUPS) * HEADS_PER_GROUP
    return 2.0 ** (-8.0 * (group * HEADS_PER_GROUP + hh + 1) / n_heads)


def _attend(q, k, v, bias, mask):
    sc = lax.dot_general(q, k, (((1,), (1,)), ((), ())),
                         preferred_element_type=F32) * (HEAD_DIM ** -0.5)
    sc = jnp.where(mask, sc + bias, NEG_INF)
    mx = jnp.max(sc, axis=-1, keepdims=True)
    e = jnp.exp(sc - mx)
    den = jnp.sum(e, axis=-1, keepdims=True)
    out = jnp.dot(e.astype(BF16), v, preferred_element_type=F32) / den
    return out, mx + jnp.log(den)


def _dil_attn_dense_kernel(q_ref, kp_ref, kc_ref, vp_ref, vc_ref, o_ref, l_ref, *, group):
    valid, valid_first, dist = _band_masks(pl.program_id(1) == 0, 1)
    for hh in range(HEADS_PER_GROUP):
        hs = slice(hh * HEAD_DIM, (hh + 1) * HEAD_DIM)
        bias = -_alibi_slope(group, hh) * dist
        for j in range(q_ref.shape[0] // Q_BLOCK):
            cur = slice(j * Q_BLOCK, (j + 1) * Q_BLOCK)
            if j == 0:
                k_prev, v_prev, mask = kp_ref[:, hs], vp_ref[:, hs], valid_first
            else:
                prev = slice((j - 1) * Q_BLOCK, j * Q_BLOCK)
                k_prev, v_prev, mask = kc_ref[prev, hs], vc_ref[prev, hs], valid
            k = jnp.concatenate([k_prev, kc_ref[cur, hs]], axis=0)
            v = jnp.concatenate([v_prev, vc_ref[cur, hs]], axis=0)
            out, lse = _attend(q_ref[cur, hs], k, v, bias, mask)
            o_ref[cur, hs] = out.astype(o_ref.dtype)
            l_ref[cur, hs] = jnp.broadcast_to(lse, (Q_BLOCK, HEAD_DIM))


RESIDUE_RADIX = 4


class _ResidueStage:
    def __init__(self, stage_ref, mid_ref, dil):
        self.stage, self.mid, self.dil = stage_ref, mid_ref, dil
        self.two_hops = dil > RESIDUE_RADIX
        assert dil == RESIDUE_RADIX or dil == RESIDUE_RADIX ** 2

    def _hop_rows(self, low):
        return pl.ds(low, self.stage.shape[0] // RESIDUE_RADIX, stride=RESIDUE_RADIX)

    def load(self, value):
        self.stage[...] = value
        if self.two_hops:
            for low in range(RESIDUE_RADIX):
                self.mid[low] = self.stage[self._hop_rows(low), :]

    def _mid_rows(self, r):
        return pl.ds(r // RESIDUE_RADIX, Q_BLOCK, stride=self.dil // RESIDUE_RADIX)

    def read(self, r):
        if self.two_hops:
            return self.mid[r % RESIDUE_RADIX, self._mid_rows(r), :]
        return self.stage[pl.ds(r, Q_BLOCK, stride=self.dil), :]

    def write(self, r, value):
        if self.two_hops:
            self.mid[r % RESIDUE_RADIX, self._mid_rows(r), :] = value
        else:
            self.stage[pl.ds(r, Q_BLOCK, stride=self.dil), :] = value

    def store(self):
        if self.two_hops:
            for low in range(RESIDUE_RADIX):
                self.stage[self._hop_rows(low), :] = self.mid[low]
        return self.stage[...]


def _dil_attn_strided_kernel(q_ref, k_ref, v_ref, o_ref, l_ref, kd_ref, vd_ref, *stage_refs,
                             dil, group):
    first_span = pl.program_id(1) == 0

    @pl.when(first_span)
    def _():
        kd_ref[...] = jnp.zeros_like(kd_ref)
        vd_ref[...] = jnp.zeros_like(vd_ref)

    refs = list(stage_refs)
    qs, ks, vs, os_, ls = (
        _ResidueStage(refs.pop(0), refs.pop(0) if dil > RESIDUE_RADIX else None, dil)
        for _ in range(5))
    _, valid_first, dist = _band_masks(first_span, dil)
    for hh in range(HEADS_PER_GROUP):
        hs = slice(hh * HEAD_DIM, (hh + 1) * HEAD_DIM)
        bias = -_alibi_slope(group, hh) * dist
        qs.load(q_ref[:, hs].astype(F32))
        ks.load(k_ref[:, hs].astype(F32))
        vs.load(v_ref[:, hs].astype(F32))

        for r in range(dil):
            k_cur = ks.read(r).astype(BF16)
            v_cur = vs.read(r).astype(BF16)
            k = jnp.concatenate([kd_ref[hh, r], k_cur], axis=0)
            v = jnp.concatenate([vd_ref[hh, r], v_cur], axis=0)
            out, lse = _attend(qs.read(r).astype(BF16), k, v, bias, valid_first)
            kd_ref[hh, r] = k_cur
            vd_ref[hh, r] = v_cur
            os_.write(r, out)
            ls.write(r, jnp.broadcast_to(lse, (Q_BLOCK, HEAD_DIM)))
        o_ref[:, hs] = os_.store().astype(o_ref.dtype)
        l_ref[:, hs] = ls.store()


def _dil_attn(proj, kv, group, *, dense_rows=512):
    _, dil = DIL_GROUPS[group]
    bsz, s, _ = proj.shape
    v_off = kv.shape[2] // (2 * GROUP_WIDTH)
    assert s % (dil * Q_BLOCK) == 0
    if dil == 1:
        rows = _row_tile(s, dense_rows)
        ratio = rows // Q_BLOCK
        prev = lambda col: pl.BlockSpec((None, Q_BLOCK, GROUP_WIDTH),
                                        lambda b, n: (b, jnp.maximum(n * ratio - 1, 0), col))
        body = functools.partial(_dil_attn_dense_kernel, group=group)
        operands, scratch = ("q", "kp", "k", "vp", "v"), []
    else:
        rows = dil * Q_BLOCK
        body = functools.partial(_dil_attn_strided_kernel, dil=dil, group=group)
        operands = ("q", "k", "v")
        split = pltpu.VMEM((HEADS_PER_GROUP, dil, Q_BLOCK, HEAD_DIM), BF16)
        stage = [pltpu.VMEM((rows, HEAD_DIM), F32)]
        if dil > RESIDUE_RADIX:
            stage.append(pltpu.VMEM((RESIDUE_RADIX, rows // RESIDUE_RADIX, HEAD_DIM), F32))
        scratch = [split, split] + stage * 5
    cur = lambda col: pl.BlockSpec((None, rows, GROUP_WIDTH), lambda b, n: (b, n, col))
    specs = {"q": cur(group), "k": cur(group), "v": cur(v_off + group)}
    if dil == 1:
        specs.update(kp=prev(group), vp=prev(v_off + group))
    return pl.pallas_call(
        body,
        grid=(bsz, s // rows),
        in_specs=[specs[name] for name in operands],
        out_specs=[cur(0), cur(0)],
        out_shape=[jax.ShapeDtypeStruct((bsz, s, GROUP_WIDTH), BF16),
                   jax.ShapeDtypeStruct((bsz, s, GROUP_WIDTH), F32)],
        scratch_shapes=scratch,
        compiler_params=_params("arbitrary", "arbitrary"),
        name=f"dil_attn_g{group}",
    )(*[proj if name == "q" else kv for name in operands])


def _mem_attn(q, kv_ref):
    outs = []
    for hh in range(MEM_HEADS):
        hs = slice(hh * HEAD_DIM, (hh + 1) * HEAD_DIM)
        vs = slice(MEM_WIDTH + hh * HEAD_DIM, MEM_WIDTH + (hh + 1) * HEAD_DIM)
        sc = lax.dot_general(q[:, hs], kv_ref[:, hs], (((1,), (1,)), ((), ())),
                             preferred_element_type=F32) * (HEAD_DIM ** -0.5)
        e = jnp.exp(sc - jnp.max(sc, axis=-1, keepdims=True))
        den = jnp.sum(e, axis=-1, keepdims=True)
        outs.append(jnp.dot(e.astype(BF16), kv_ref[:, vs], preferred_element_type=F32) / den)
    return jnp.concatenate(outs, axis=-1)


def _mix_out_tail(mixer, q_ref, kv_ref, w_ref, g_ref, h_ref, o_ref):
    m = _mem_attn(q_ref[...], kv_ref)
    mix = jnp.concatenate([mixer, m.astype(BF16)], axis=-1)
    out = jnp.dot(mix, w_ref[...], preferred_element_type=F32)
    o_ref[...] = h_ref[...] + _rms(out, g_ref[...])


def _mix_out_rec_kernel(y_ref, q_ref, kv_ref, w_ref, g_ref, h_ref, o_ref):
    _mix_out_tail(y_ref[...], q_ref, kv_ref, w_ref, g_ref, h_ref, o_ref)


def _mix_out_dil_kernel(o0_ref, o1_ref, o2_ref, l0_ref, l1_ref, l2_ref,
                        q_ref, kv_ref, w_ref, g_ref, h_ref, o_ref):
    lses = [l0_ref[...], l1_ref[...], l2_ref[...]]
    mx = jnp.maximum(jnp.maximum(lses[0], lses[1]), lses[2])
    es = [jnp.exp(l - mx) for l in lses]
    den = es[0] + es[1] + es[2]
    parts = [(o_ref_g[...].astype(F32) * (e / den)).astype(BF16)
             for o_ref_g, e in zip((o0_ref, o1_ref, o2_ref), es)]
    _mix_out_tail(jnp.concatenate(parts, axis=-1), q_ref, kv_ref, w_ref, g_ref, h_ref, o_ref)


def _mix_out(mixer_inputs, proj, memkv, w_out, post_g, h, layer, *, tm=512):
    bsz, s, d = h.shape
    tm = _row_tile(s, tm)
    q_block = proj.shape[2] // MEM_WIDTH - 1
    row_spec = lambda width, col=0: pl.BlockSpec((None, tm, width), lambda b, i: (b, i, col))
    kernel = _mix_out_rec_kernel if len(mixer_inputs) == 1 else _mix_out_dil_kernel
    return pl.pallas_call(
        kernel,
        grid=(bsz, s // tm),
        in_specs=[row_spec(a.shape[2]) for a in mixer_inputs] + [
            row_spec(MEM_WIDTH, q_block),
            pl.BlockSpec((None,) + memkv.shape[1:], lambda b, i: (b, 0, 0)),
            _resident((None,) + w_out.shape[1:], lambda b, i: (layer, 0, 0)),
            _resident((1, d), lambda b, i: (0, 0)),
            row_spec(d),
        ],
        out_specs=row_spec(d),
        out_shape=jax.ShapeDtypeStruct((bsz, s, d), F32),
        compiler_params=_params("arbitrary", "arbitrary"),
        name="mix_out_rec" if len(mixer_inputs) == 1 else "mix_out_dil",
    )(*mixer_inputs, proj, memkv, w_out, post_g.reshape(1, d), h)


def kernel(x, mem, mem_norm_g, a_pre_mix_g, a_post_mix_g, a_pre_ffn_g, a_post_ffn_g, a_w_in, a_conv_w, a_conv_b, a_gate_a_w, a_gate_a_b, a_gate_x_w, a_gate_x_b, a_lambda, a_w_mem_kv, a_w_out, a_w_ffn_in, a_w_ffn_out, kv_norm_g, w_kv_shared, b_pre_mix_g, b_post_mix_g, b_pre_ffn_g, b_post_ffn_g, b_w_in, b_w_mem_kv, b_w_out, b_w_ffn_in, b_w_ffn_out):
    bsz, s, d = x.shape
    n_mem = mem.shape[1]
    n_a = a_w_in.shape[0]
    n_b = b_w_in.shape[0]
    bf = lambda w: w.astype(BF16)
    a_w_in, a_gate_a_w, a_gate_x_w, a_w_mem_kv, a_w_out, a_w_ffn_in, a_w_ffn_out = map(
        bf, (a_w_in, a_gate_a_w, a_gate_x_w, a_w_mem_kv, a_w_out, a_w_ffn_in, a_w_ffn_out))
    b_w_in, b_w_mem_kv, b_w_out, b_w_ffn_in, b_w_ffn_out = map(
        bf, (b_w_in, b_w_mem_kv, b_w_out, b_w_ffn_in, b_w_ffn_out))
    w_kv_shared = bf(w_kv_shared)[None]

    def flat(t):
        return t.reshape(bsz * s, t.shape[-1])

    def mem_kv(w, l):
        kv_mem = _norm_matmul(mem.reshape(bsz * n_mem, d), mem_norm_g, w, l)
        return kv_mem.reshape(bsz, n_mem, -1)

    def ffn(h, pre_g, post_g, w_in, w_out, l):
        return _ffn(flat(h), pre_g, post_g, w_in, w_out, l).reshape(bsz, s, d)

    h = x
    for l in range(n_a):
        proj = _norm_matmul(flat(h), a_pre_mix_g[l], a_w_in, l).reshape(bsz, s, -1)
        y = _rglru(proj, a_conv_w[l], a_conv_b[l], a_gate_a_w, a_gate_a_b[l],
                   a_gate_x_w, a_gate_x_b[l], a_lambda[l], l)
        h = _mix_out([y], proj, mem_kv(a_w_mem_kv, l), a_w_out, a_post_mix_g[l], h, l)
        h = ffn(h, a_pre_ffn_g[l], a_post_ffn_g[l], a_w_ffn_in, a_w_ffn_out, l)

    kv = _norm_matmul(flat(h), kv_norm_g, w_kv_shared, 0).reshape(bsz, s, -1)
    for l in range(n_b):
        proj = _norm_matmul(flat(h), b_pre_mix_g[l], b_w_in, l).reshape(bsz, s, -1)
        outs, lses = zip(*[_dil_attn(proj, kv, g) for g in range(len(DIL_GROUPS))])
        h = _mix_out(list(outs) + list(lses), proj, mem_kv(b_w_mem_kv, l), b_w_out,
                     b_post_mix_g[l], h, l)
        h = ffn(h, b_pre_ffn_g[l], b_post_ffn_g[l], b_w_ffn_in, b_w_ffn_out, l)
    return h
```

```python
import functools
import math

import jax
import jax.numpy as jnp
from jax import lax
from jax.experimental import pallas as pl
from jax.experimental.pallas import tpu as pltpu

HEAD_DIM = 128
MEM_HEADS = 4
MEM_WIDTH = MEM_HEADS * HEAD_DIM
CONV_WIDTH = 4
LRU_C = 8.0
DIL_GROUPS = ((128, 1), (512, 4), (2048, 16))
HEADS_PER_GROUP = 4
GROUP_WIDTH = HEADS_PER_GROUP * HEAD_DIM
Q_BLOCK = 128
RMS_EPS = 1e-6
NEG_INF = -1e30

SUBLANES = 8
NORM_CHUNK_ROWS = 2 * SUBLANES
V7X_VMEM_BYTES = 64 * 1024 * 1024
VMEM_LIMIT_BYTES = V7X_VMEM_BYTES - 8 * 1024 * 1024
FFN_VMEM_LIMIT_BYTES = V7X_VMEM_BYTES - 2 * 1024 * 1024

F32 = jnp.float32
BF16 = jnp.bfloat16


def _params(*semantics, vmem_limit_bytes=VMEM_LIMIT_BYTES):
    return pltpu.CompilerParams(dimension_semantics=semantics, vmem_limit_bytes=vmem_limit_bytes)


def _rms(x, g):
    return x * lax.rsqrt(jnp.mean(x * x, axis=-1, keepdims=True) + RMS_EPS) * g


def _row_tile(rows, target):
    t = min(rows, target)
    assert rows % t == 0, (rows, t)
    return t


BF16_ROW_TILE = 16


def _side_cast_specs(weights, layer, n_steps, step_of):
    in_specs, out_specs, out_shapes = [], [], []
    for w in weights:
        _, rows, cols = w.shape
        slab = rows // n_steps
        assert rows % n_steps == 0 and slab % BF16_ROW_TILE == 0, (w.shape, n_steps)
        in_specs.append(pl.BlockSpec((None, slab, cols), lambda *ids: (layer, step_of(*ids), 0)))
        out_specs.append(pl.BlockSpec((None, slab, cols), lambda *ids: (0, step_of(*ids), 0)))
        out_shapes.append(jax.ShapeDtypeStruct((1, rows, cols), BF16))
    return in_specs, out_specs, out_shapes


def _side_cast(src_refs, dst_refs):
    for src, dst in zip(src_refs, dst_refs):
        dst[...] = src[...].astype(dst.dtype)


def _norm_matmul_kernel(h_ref, g_ref, w_ref, *refs, tn):
    n_cast = len(refs) // 2
    o_ref = refs[n_cast]
    _side_cast(refs[:n_cast], refs[n_cast + 1:])
    hn = _rms(h_ref[...], g_ref[...]).astype(BF16)
    for c in range(0, o_ref.shape[1], tn):
        o_ref[:, c:c + tn] = jnp.dot(hn, w_ref[:, c:c + tn],
                                     preferred_element_type=F32).astype(o_ref.dtype)


def _resident(block_shape, index_map):
    return pl.BlockSpec(block_shape, index_map, pipeline_mode=pl.Buffered(1))


def _norm_matmul(h, g, w, layer, cast_weights=(), *, tm=512, tn=512):
    m, d = h.shape
    n = w.shape[2]
    tm = _row_tile(m, tm)
    tn = _row_tile(n, tn)
    cast_in, cast_out, cast_shapes = _side_cast_specs(cast_weights, layer, m // tm, lambda i: i)
    out = pl.pallas_call(
        functools.partial(_norm_matmul_kernel, tn=tn),
        grid=(m // tm,),
        in_specs=[
            pl.BlockSpec((tm, d), lambda i: (i, 0)),
            _resident((1, d), lambda i: (0, 0)),
            _resident((None, d, n), lambda i: (layer, 0, 0)),
        ] + cast_in,
        out_specs=[pl.BlockSpec((tm, n), lambda i: (i, 0))] + cast_out,
        out_shape=[jax.ShapeDtypeStruct((m, n), BF16)] + cast_shapes,
        compiler_params=_params("arbitrary"),
        name="norm_matmul",
    )(h, g.reshape(1, d), w, *cast_weights)
    return (out[0], out[1:]) if cast_weights else out[0]


def _ffn_kernel(h_ref, pre_g_ref, post_g_ref, wg_ref, wu_ref, wo_ref, o_ref, hn_ref):
    j = pl.program_id(1)

    @pl.when(j == 0)
    def _():
        hn_ref[...] = _rms(h_ref[...], pre_g_ref[...]).astype(hn_ref.dtype)
        o_ref[...] = jnp.zeros_like(o_ref)

    hn = hn_ref[...]
    gate = jnp.dot(hn, wg_ref[...], preferred_element_type=F32)
    up = jnp.dot(hn, wu_ref[...], preferred_element_type=F32)
    act = (gate * jax.nn.sigmoid(gate) * up).astype(BF16)
    o_ref[...] += jnp.dot(act, wo_ref[...], preferred_element_type=F32)

    @pl.when(j == pl.num_programs(1) - 1)
    def _():
        for r in range(0, o_ref.shape[0], NORM_CHUNK_ROWS):
            rows = slice(r, r + NORM_CHUNK_ROWS)
            o_ref[rows, :] = h_ref[rows, :] + _rms(o_ref[rows, :], post_g_ref[...])


def _ffn(h, pre_g, post_g, w_in, w_out, layer, *, tm=1024, tf=512):
    m, d = h.shape
    f = w_out.shape[1]
    tm = _row_tile(m, tm)
    tf = _row_tile(f, tf)
    nf = f // tf
    return pl.pallas_call(
        _ffn_kernel,
        grid=(m // tm, nf),
        in_specs=[
            pl.BlockSpec((tm, d), lambda i, j: (i, 0)),
            pl.BlockSpec((1, d), lambda i, j: (0, 0)),
            pl.BlockSpec((1, d), lambda i, j: (0, 0)),
            pl.BlockSpec((None, d, tf), lambda i, j: (layer, 0, j)),
            pl.BlockSpec((None, d, tf), lambda i, j: (layer, 0, j + nf)),
            pl.BlockSpec((None, tf, d), lambda i, j: (layer, j, 0)),
        ],
        out_specs=pl.BlockSpec((tm, d), lambda i, j: (i, 0)),
        out_shape=jax.ShapeDtypeStruct((m, d), F32),
        scratch_shapes=[pltpu.VMEM((tm, d), BF16)],
        compiler_params=_params("arbitrary", "arbitrary", vmem_limit_bytes=FFN_VMEM_LIMIT_BYTES),
        name="ffn",
    )(h, pre_g.reshape(1, d), post_g.reshape(1, d), w_in, w_in, w_out)


LRU_PAD = SUBLANES


def _lru_steps(first, count):
    return pl.ds(2 * (LRU_PAD + first), count, stride=2)


def _rglru_reset(xs_ref, hprev_ref, ts):
    hprev_ref[...] = jnp.zeros_like(hprev_ref)
    for n in range(xs_ref.shape[0]):
        xs_ref[n, _lru_steps(ts - LRU_PAD, LRU_PAD), :] = jnp.zeros((LRU_PAD, HEAD_DIM), F32)


def _rglru_tile(x_of, gate_of, y_ref, cw_ref, cb_ref, wa_ref, ba_ref, wx_ref, bx_ref, lam_ref,
                xs_ref, as_ref, bs_ref, hprev_ref, ts):
    pad, steps = LRU_PAD, _lru_steps
    n_blocks = cw_ref.shape[1] // HEAD_DIM

    for n in range(n_blocks):
        xs_ref[n, steps(-pad, pad), :] = xs_ref[n, steps(ts - pad, pad), :]
        as_ref[n, steps(-pad, pad), :] = jnp.ones((pad, HEAD_DIM), F32)
        bs_ref[n, steps(-pad, pad), :] = jnp.zeros((pad, HEAD_DIM), F32)

    for n in range(n_blocks):
        sl = slice(n * HEAD_DIM, (n + 1) * HEAD_DIM)
        x = x_of(n).astype(F32)
        xs_ref[n, steps(0, ts), :] = x
        cw = cw_ref[:, sl]
        xc = cb_ref[:, sl] + cw[CONV_WIDTH - 1:CONV_WIDTH, :] * x
        for s in range(1, CONV_WIDTH):
            xc = xc + cw[CONV_WIDTH - 1 - s:CONV_WIDTH - s, :] * xs_ref[n, steps(-s, ts), :]

        xcb = xc.astype(BF16)
        r = jax.nn.sigmoid(jnp.dot(xcb, wa_ref[n], preferred_element_type=F32) + ba_ref[:, sl])
        i = jax.nn.sigmoid(jnp.dot(xcb, wx_ref[n], preferred_element_type=F32) + bx_ref[:, sl])
        neg_lam = -lam_ref[:, sl]
        softplus = jnp.maximum(neg_lam, 0.0) + jnp.log1p(jnp.exp(-jnp.abs(neg_lam)))
        a = jnp.exp2(r * ((-LRU_C * math.log2(math.e)) * softplus))
        z = 1.0 - a * a
        b = jnp.where(z > 0.0, z * lax.rsqrt(z), 0.0) * (i * xc)

        for shift in (1, 2, 4):
            as_ref[n, steps(0, ts), :] = a
            bs_ref[n, steps(0, ts), :] = b
            b = a * bs_ref[n, steps(-shift, ts), :] + b
            a = a * as_ref[n, steps(-shift, ts), :]
        hg = a[:SUBLANES] * hprev_ref[:, sl] + b[:SUBLANES]
        groups = [hg]
        for v in range(1, ts // SUBLANES):
            rows = slice(v * SUBLANES, (v + 1) * SUBLANES)
            hg = a[rows] * hg + b[rows]
            groups.append(hg)
        h = jnp.concatenate(groups, axis=0)
        hprev_ref[:, sl] = hg[SUBLANES - 1:, :]

        gb = gate_of(n).astype(F32)
        k = math.sqrt(2.0 / math.pi)
        cdf = 0.5 * (1.0 + jnp.tanh(gb * (k + (k * 0.044715) * (gb * gb))))
        y_ref[:, sl] = (h * (gb * cdf)).astype(y_ref.dtype)


def _rglru_kernel(xb_ref, gb_ref, cw_ref, cb_ref, wa_ref, ba_ref, wx_ref, bx_ref, lam_ref,
                  *refs, n_cast):
    cast_in, y_ref, cast_out = refs[:n_cast], refs[n_cast], refs[n_cast + 1:2 * n_cast + 1]
    xs_ref, as_ref, bs_ref, hprev_ref = refs[2 * n_cast + 1:]
    ts = xb_ref.shape[0]
    _side_cast(cast_in, cast_out)

    @pl.when(pl.program_id(1) == 0)
    def _():
        _rglru_reset(xs_ref, hprev_ref, ts)

    _rglru_tile(lambda n: xb_ref[:, n * HEAD_DIM:(n + 1) * HEAD_DIM],
                lambda n: gb_ref[:, n * HEAD_DIM:(n + 1) * HEAD_DIM],
                y_ref, cw_ref, cb_ref, wa_ref, ba_ref, wx_ref, bx_ref, lam_ref,
                xs_ref, as_ref, bs_ref, hprev_ref, ts)


def _rglru(proj, conv_w, conv_b, wa, ba, wx, bx, lam, layer, cast_weights=(), *, ts=512):
    bsz, s, _ = proj.shape
    c = conv_w.shape[1]
    ts = _row_tile(s, ts)
    nt = s // ts
    vec = lambda v: v.reshape(1, c)
    full2 = lambda shape: pl.BlockSpec(shape, lambda b, i: (0, 0))
    gate_w = lambda w: pl.BlockSpec((None,) + w.shape[1:], lambda b, i: (layer, 0, 0, 0))
    stage = pltpu.VMEM((c // HEAD_DIM, 2 * (ts + LRU_PAD), HEAD_DIM), F32)
    cast_in, cast_out, cast_shapes = _side_cast_specs(cast_weights, layer, bsz * nt,
                                                      lambda b, i: b * nt + i)
    out = pl.pallas_call(
        functools.partial(_rglru_kernel, n_cast=len(cast_weights)),
        grid=(bsz, nt),
        in_specs=[
            pl.BlockSpec((None, ts, c), lambda b, i: (b, i, 0)),
            pl.BlockSpec((None, ts, c), lambda b, i: (b, i, 1)),
            full2((CONV_WIDTH, c)), full2((1, c)),
            gate_w(wa), full2((1, c)),
            gate_w(wx), full2((1, c)),
            full2((1, c)),
        ] + cast_in,
        out_specs=[pl.BlockSpec((None, ts, c), lambda b, i: (b, i, 0))] + cast_out,
        out_shape=[jax.ShapeDtypeStruct((bsz, s, c), BF16)] + cast_shapes,
        scratch_shapes=[stage, stage, stage, pltpu.VMEM((1, c), F32)],
        compiler_params=_params("arbitrary", "arbitrary"),
        name="rglru",
    )(proj, proj, conv_w, vec(conv_b), wa, vec(ba), wx, vec(bx), vec(lam), *cast_weights)
    return out[0], out[1:]


def _band_masks(first_block, dil):
    qi = lax.broadcasted_iota(jnp.int32, (Q_BLOCK, 2 * Q_BLOCK), 0)
    ki = lax.broadcasted_iota(jnp.int32, (Q_BLOCK, 2 * Q_BLOCK), 1)
    rel = qi + Q_BLOCK - ki
    valid = (rel >= 0) & (rel <= Q_BLOCK)
    valid_first = valid & (ki >= jnp.where(first_block, Q_BLOCK, 0))
    return valid, valid_first, (rel * dil).astype(F32)


def _alibi_slope(group, hh):
    n_heads = len(DIL_GROUPS) * HEADS_PER_GROUP
    return 2.0 ** (-8.0 * (group * HEADS_PER_GROUP + hh + 1) / n_heads)


def _attend(q, k, v, bias, mask):
    sc = lax.dot_general(q, k, (((1,), (1,)), ((), ())),
                         preferred_element_type=F32) * (HEAD_DIM ** -0.5)
    sc = jnp.where(mask, sc + bias, NEG_INF)
    mx = jnp.max(sc, axis=-1, keepdims=True)
    e = jnp.exp(sc - mx)
    den = jnp.sum(e, axis=-1, keepdims=True)
    out = jnp.dot(e.astype(BF16), v, preferred_element_type=F32) / den
    return out, mx + jnp.log(den)


def _dil_attn_dense_kernel(q_ref, kp_ref, kc_ref, vp_ref, vc_ref, o_ref, l_ref, *, group):
    valid, valid_first, dist = _band_masks(pl.program_id(1) == 0, 1)
    for hh in range(HEADS_PER_GROUP):
        hs = slice(hh * HEAD_DIM, (hh + 1) * HEAD_DIM)
        bias = -_alibi_slope(group, hh) * dist
        for j in range(q_ref.shape[0] // Q_BLOCK):
            cur = slice(j * Q_BLOCK, (j + 1) * Q_BLOCK)
            if j == 0:
                k_prev, v_prev, mask = kp_ref[:, hs], vp_ref[:, hs], valid_first
            else:
                prev = slice((j - 1) * Q_BLOCK, j * Q_BLOCK)
                k_prev, v_prev, mask = kc_ref[prev, hs], vc_ref[prev, hs], valid
            k = jnp.concatenate([k_prev, kc_ref[cur, hs]], axis=0)
            v = jnp.concatenate([v_prev, vc_ref[cur, hs]], axis=0)
            out, lse = _attend(q_ref[cur, hs], k, v, bias, mask)
            o_ref[cur, hs] = out.astype(o_ref.dtype)
            l_ref[cur, hs] = jnp.broadcast_to(lse, (Q_BLOCK, HEAD_DIM))


RESIDUE_RADIX = 4


class _ResidueStage:
    def __init__(self, stage_ref, mid_ref, dil):
        self.stage, self.mid, self.dil = stage_ref, mid_ref, dil
        self.two_hops = dil > RESIDUE_RADIX
        assert dil == RESIDUE_RADIX or dil == RESIDUE_RADIX ** 2

    def _hop_rows(self, low):
        return pl.ds(low, self.stage.shape[0] // RESIDUE_RADIX, stride=RESIDUE_RADIX)

    def load(self, value):
        self.stage[...] = value
        if self.two_hops:
            for low in range(RESIDUE_RADIX):
                self.mid[low] = self.stage[self._hop_rows(low), :]

    def _mid_rows(self, r):
        return pl.ds(r // RESIDUE_RADIX, Q_BLOCK, stride=self.dil // RESIDUE_RADIX)

    def read(self, r):
        if self.two_hops:
            return self.mid[r % RESIDUE_RADIX, self._mid_rows(r), :]
        return self.stage[pl.ds(r, Q_BLOCK, stride=self.dil), :]

    def write(self, r, value):
        if self.two_hops:
            self.mid[r % RESIDUE_RADIX, self._mid_rows(r), :] = value
        else:
            self.stage[pl.ds(r, Q_BLOCK, stride=self.dil), :] = value

    def store(self):
        if self.two_hops:
            for low in range(RESIDUE_RADIX):
                self.stage[self._hop_rows(low), :] = self.mid[low]
        return self.stage[...]


def _dil_attn_strided_kernel(q_ref, k_ref, v_ref, o_ref, l_ref, kd_ref, vd_ref, *stage_refs,
                             dil, group):
    first_span = pl.program_id(1) == 0

    @pl.when(first_span)
    def _():
        kd_ref[...] = jnp.zeros_like(kd_ref)
        vd_ref[...] = jnp.zeros_like(vd_ref)

    refs = list(stage_refs)
    qs, ks, vs, os_, ls = (
        _ResidueStage(refs.pop(0), refs.pop(0) if dil > RESIDUE_RADIX else None, dil)
        for _ in range(5))
    _, valid_first, dist = _band_masks(first_span, dil)
    for hh in range(HEADS_PER_GROUP):
        hs = slice(hh * HEAD_DIM, (hh + 1) * HEAD_DIM)
        bias = -_alibi_slope(group, hh) * dist
        qs.load(q_ref[:, hs].astype(F32))
        ks.load(k_ref[:, hs].astype(F32))
        vs.load(v_ref[:, hs].astype(F32))

        for r in range(dil):
            k_cur = ks.read(r).astype(BF16)
            v_cur = vs.read(r).astype(BF16)
            k = jnp.concatenate([kd_ref[hh, r], k_cur], axis=0)
            v = jnp.concatenate([vd_ref[hh, r], v_cur], axis=0)
            out, lse = _attend(qs.read(r).astype(BF16), k, v, bias, valid_first)
            kd_ref[hh, r] = k_cur
            vd_ref[hh, r] = v_cur
            os_.write(r, out)
            ls.write(r, jnp.broadcast_to(lse, (Q_BLOCK, HEAD_DIM)))
        o_ref[:, hs] = os_.store().astype(o_ref.dtype)
        l_ref[:, hs] = ls.store()


def _dil_attn(proj, kv, group, *, dense_rows=512):
    _, dil = DIL_GROUPS[group]
    bsz, s, _ = proj.shape
    v_off = kv.shape[2] // (2 * GROUP_WIDTH)
    assert s % (dil * Q_BLOCK) == 0
    if dil == 1:
        rows = _row_tile(s, dense_rows)
        ratio = rows // Q_BLOCK
        prev = lambda col: pl.BlockSpec((None, Q_BLOCK, GROUP_WIDTH),
                                        lambda b, n: (b, jnp.maximum(n * ratio - 1, 0), col))
        body = functools.partial(_dil_attn_dense_kernel, group=group)
        operands, scratch = ("q", "kp", "k", "vp", "v"), []
    else:
        rows = dil * Q_BLOCK
        body = functools.partial(_dil_attn_strided_kernel, dil=dil, group=group)
        operands = ("q", "k", "v")
        split = pltpu.VMEM((HEADS_PER_GROUP, dil, Q_BLOCK, HEAD_DIM), BF16)
        stage = [pltpu.VMEM((rows, HEAD_DIM), F32)]
        if dil > RESIDUE_RADIX:
            stage.append(pltpu.VMEM((RESIDUE_RADIX, rows // RESIDUE_RADIX, HEAD_DIM), F32))
        scratch = [split, split] + stage * 5
    cur = lambda col: pl.BlockSpec((None, rows, GROUP_WIDTH), lambda b, n: (b, n, col))
    specs = {"q": cur(group), "k": cur(group), "v": cur(v_off + group)}
    if dil == 1:
        specs.update(kp=prev(group), vp=prev(v_off + group))
    return pl.pallas_call(
        body,
        grid=(bsz, s // rows),
        in_specs=[specs[name] for name in operands],
        out_specs=[cur(0), cur(0)],
        out_shape=[jax.ShapeDtypeStruct((bsz, s, GROUP_WIDTH), BF16),
                   jax.ShapeDtypeStruct((bsz, s, GROUP_WIDTH), F32)],
        scratch_shapes=scratch,
        compiler_params=_params("arbitrary", "arbitrary"),
        name=f"dil_attn_g{group}",
    )(*[proj if name == "q" else kv for name in operands])


def _mem_attn(q, kv_ref):
    outs = []
    for hh in range(MEM_HEADS):
        hs = slice(hh * HEAD_DIM, (hh + 1) * HEAD_DIM)
        vs = slice(MEM_WIDTH + hh * HEAD_DIM, MEM_WIDTH + (hh + 1) * HEAD_DIM)
        sc = lax.dot_general(q[:, hs], kv_ref[:, hs], (((1,), (1,)), ((), ())),
                             preferred_element_type=F32) * (HEAD_DIM ** -0.5)
        e = jnp.exp(sc - jnp.max(sc, axis=-1, keepdims=True))
        den = jnp.sum(e, axis=-1, keepdims=True)
        outs.append(jnp.dot(e.astype(BF16), kv_ref[:, vs], preferred_element_type=F32) / den)
    return jnp.concatenate(outs, axis=-1)


def _mix_out_tail(mixer, q_ref, kv_ref, w_ref, g_ref, h_ref, o_ref):
    m = _mem_attn(q_ref[...], kv_ref)
    mix = jnp.concatenate([mixer, m.astype(BF16)], axis=-1)
    out = jnp.dot(mix, w_ref[...], preferred_element_type=F32)
    o_ref[...] = h_ref[...] + _rms(out, g_ref[...])


def _mix_out_rec_kernel(y_ref, q_ref, kv_ref, w_ref, g_ref, h_ref, o_ref):
    _mix_out_tail(y_ref[...], q_ref, kv_ref, w_ref, g_ref, h_ref, o_ref)


def _mix_out_dil_kernel(o0_ref, o1_ref, o2_ref, l0_ref, l1_ref, l2_ref,
                        q_ref, kv_ref, w_ref, g_ref, h_ref, o_ref):
    lses = [l0_ref[...], l1_ref[...], l2_ref[...]]
    mx = jnp.maximum(jnp.maximum(lses[0], lses[1]), lses[2])
    es = [jnp.exp(l - mx) for l in lses]
    den = es[0] + es[1] + es[2]
    parts = [(o_ref_g[...].astype(F32) * (e / den)).astype(BF16)
             for o_ref_g, e in zip((o0_ref, o1_ref, o2_ref), es)]
    _mix_out_tail(jnp.concatenate(parts, axis=-1), q_ref, kv_ref, w_ref, g_ref, h_ref, o_ref)


def _mix_out(mixer_inputs, proj, memkv, w_out, post_g, h, layer, *, tm=512):
    bsz, s, d = h.shape
    tm = _row_tile(s, tm)
    q_block = proj.shape[2] // MEM_WIDTH - 1
    row_spec = lambda width, col=0: pl.BlockSpec((None, tm, width), lambda b, i: (b, i, col))
    kernel = _mix_out_rec_kernel if len(mixer_inputs) == 1 else _mix_out_dil_kernel
    return pl.pallas_call(
        kernel,
        grid=(bsz, s // tm),
        in_specs=[row_spec(a.shape[2]) for a in mixer_inputs] + [
            row_spec(MEM_WIDTH, q_block),
            pl.BlockSpec((None,) + memkv.shape[1:], lambda b, i: (b, 0, 0)),
            _resident((None,) + w_out.shape[1:], lambda b, i: (layer, 0, 0)),
            _resident((1, d), lambda b, i: (0, 0)),
            row_spec(d),
        ],
        out_specs=row_spec(d),
        out_shape=jax.ShapeDtypeStruct((bsz, s, d), F32),
        compiler_params=_params("arbitrary", "arbitrary"),
        name="mix_out_rec" if len(mixer_inputs) == 1 else "mix_out_dil",
    )(*mixer_inputs, proj, memkv, w_out, post_g.reshape(1, d), h)


def kernel(x, mem, mem_norm_g, a_pre_mix_g, a_post_mix_g, a_pre_ffn_g, a_post_ffn_g, a_w_in, a_conv_w, a_conv_b, a_gate_a_w, a_gate_a_b, a_gate_x_w, a_gate_x_b, a_lambda, a_w_mem_kv, a_w_out, a_w_ffn_in, a_w_ffn_out, kv_norm_g, w_kv_shared, b_pre_mix_g, b_post_mix_g, b_pre_ffn_g, b_post_ffn_g, b_w_in, b_w_mem_kv, b_w_out, b_w_ffn_in, b_w_ffn_out):
    bsz, s, d = x.shape
    n_mem = mem.shape[1]
    n_a = a_w_in.shape[0]
    n_b = b_w_in.shape[0]
    bf = lambda w: w.astype(BF16)
    a_w_in, a_gate_a_w, a_gate_x_w, a_w_mem_kv, a_w_out = map(
        bf, (a_w_in, a_gate_a_w, a_gate_x_w, a_w_mem_kv, a_w_out))
    b_w_in, b_w_mem_kv, b_w_out = map(bf, (b_w_in, b_w_mem_kv, b_w_out))
    w_kv_shared = bf(w_kv_shared)[None]

    def flat(t):
        return t.reshape(bsz * s, t.shape[-1])

    def mem_kv(w, l):
        kv_mem = _norm_matmul(mem.reshape(bsz * n_mem, d), mem_norm_g, w, l)
        return kv_mem.reshape(bsz, n_mem, -1)

    def ffn(h, pre_g, post_g, w_in, w_out):
        return _ffn(flat(h), pre_g, post_g, w_in, w_out, 0).reshape(bsz, s, d)

    h = x
    for l in range(n_a):
        proj = _norm_matmul(flat(h), a_pre_mix_g[l], a_w_in, l).reshape(bsz, s, -1)
        y, ffn_w = _rglru(proj, a_conv_w[l], a_conv_b[l], a_gate_a_w, a_gate_a_b[l],
                          a_gate_x_w, a_gate_x_b[l], a_lambda[l], l,
                          cast_weights=(a_w_ffn_in, a_w_ffn_out))
        h = _mix_out([y], proj, mem_kv(a_w_mem_kv, l), a_w_out, a_post_mix_g[l], h, l)
        h = ffn(h, a_pre_ffn_g[l], a_post_ffn_g[l], *ffn_w)

    kv = _norm_matmul(flat(h), kv_norm_g, w_kv_shared, 0).reshape(bsz, s, -1)
    for l in range(n_b):
        proj, ffn_w = _norm_matmul(flat(h), b_pre_mix_g[l], b_w_in, l,
                                   cast_weights=(b_w_ffn_in, b_w_ffn_out))
        proj = proj.reshape(bsz, s, -1)
        outs, lses = zip(*[_dil_attn(proj, kv, g) for g in range(len(DIL_GROUPS))])
        h = _mix_out(list(outs) + list(lses), proj, mem_kv(b_w_mem_kv, l), b_w_out,
                     b_post_mix_g[l], h, l)
        h = ffn(h, b_pre_ffn_g[l], b_post_ffn_g[l], *ffn_w)
    return h
```

```python
import functools
import math

import jax
import jax.numpy as jnp
from jax import lax
from jax.experimental import pallas as pl
from jax.experimental.pallas import tpu as pltpu

HEAD_DIM = 128
MEM_HEADS = 4
MEM_WIDTH = MEM_HEADS * HEAD_DIM
CONV_WIDTH = 4
LRU_C = 8.0
DIL_GROUPS = ((128, 1), (512, 4), (2048, 16))
HEADS_PER_GROUP = 4
GROUP_WIDTH = HEADS_PER_GROUP * HEAD_DIM
Q_BLOCK = 128
RMS_EPS = 1e-6
NEG_INF = -1e30

SUBLANES = 8
NORM_CHUNK_ROWS = 2 * SUBLANES
V7X_VMEM_BYTES = 64 * 1024 * 1024
VMEM_LIMIT_BYTES = V7X_VMEM_BYTES - 8 * 1024 * 1024
FFN_VMEM_LIMIT_BYTES = V7X_VMEM_BYTES - 2 * 1024 * 1024

F32 = jnp.float32
BF16 = jnp.bfloat16


def _params(*semantics, vmem_limit_bytes=VMEM_LIMIT_BYTES):
    return pltpu.CompilerParams(dimension_semantics=semantics, vmem_limit_bytes=vmem_limit_bytes)


def _rms(x, g):
    return x * lax.rsqrt(jnp.mean(x * x, axis=-1, keepdims=True) + RMS_EPS) * g


def _row_tile(rows, target):
    t = min(rows, target)
    assert rows % t == 0, (rows, t)
    return t


BF16_ROW_TILE = 16


def _side_cast_specs(casts, n_steps, step_of):
    in_specs, out_specs, out_shapes = [], [], []
    for w, layer in casts:
        _, rows, cols = w.shape
        slab = rows // n_steps
        assert rows % n_steps == 0 and slab % BF16_ROW_TILE == 0, (w.shape, n_steps)
        in_specs.append(pl.BlockSpec(
            (None, slab, cols), lambda *ids, layer=layer: (layer, step_of(*ids), 0)))
        out_specs.append(pl.BlockSpec((None, slab, cols), lambda *ids: (0, step_of(*ids), 0)))
        out_shapes.append(jax.ShapeDtypeStruct((1, rows, cols), BF16))
    return in_specs, out_specs, out_shapes


def _side_cast(src_refs, dst_refs):
    for src, dst in zip(src_refs, dst_refs):
        dst[...] = src[...].astype(dst.dtype)


def _norm_matmul_kernel(h_ref, g_ref, w_ref, *refs, tn):
    n_cast = len(refs) // 2
    o_ref = refs[n_cast]
    _side_cast(refs[:n_cast], refs[n_cast + 1:])
    hn = _rms(h_ref[...], g_ref[...]).astype(BF16)
    for c in range(0, o_ref.shape[1], tn):
        o_ref[:, c:c + tn] = jnp.dot(hn, w_ref[:, c:c + tn],
                                     preferred_element_type=F32).astype(o_ref.dtype)


def _resident(block_shape, index_map):
    return pl.BlockSpec(block_shape, index_map, pipeline_mode=pl.Buffered(1))


def _norm_matmul(h, g, w, layer, casts=(), *, tm=512, tn=512):
    m, d = h.shape
    n = w.shape[2]
    tm = _row_tile(m, tm)
    tn = _row_tile(n, tn)
    cast_in, cast_out, cast_shapes = _side_cast_specs(casts, m // tm, lambda i: i)
    out = pl.pallas_call(
        functools.partial(_norm_matmul_kernel, tn=tn),
        grid=(m // tm,),
        in_specs=[
            pl.BlockSpec((tm, d), lambda i: (i, 0)),
            _resident((1, d), lambda i: (0, 0)),
            _resident((None, d, n), lambda i: (layer, 0, 0)),
        ] + cast_in,
        out_specs=[pl.BlockSpec((tm, n), lambda i: (i, 0))] + cast_out,
        out_shape=[jax.ShapeDtypeStruct((m, n), BF16)] + cast_shapes,
        compiler_params=_params("arbitrary"),
        name="norm_matmul",
    )(h, g.reshape(1, d), w, *[cw for cw, _ in casts])
    return (out[0], out[1:]) if casts else out[0]


def _ffn_kernel(h_ref, pre_g_ref, post_g_ref, wg_ref, wu_ref, wo_ref, o_ref, hn_ref):
    j = pl.program_id(1)

    @pl.when(j == 0)
    def _():
        hn_ref[...] = _rms(h_ref[...], pre_g_ref[...]).astype(hn_ref.dtype)
        o_ref[...] = jnp.zeros_like(o_ref)

    hn = hn_ref[...]
    gate = jnp.dot(hn, wg_ref[...], preferred_element_type=F32)
    up = jnp.dot(hn, wu_ref[...], preferred_element_type=F32)
    act = (gate * jax.nn.sigmoid(gate) * up).astype(BF16)
    o_ref[...] += jnp.dot(act, wo_ref[...], preferred_element_type=F32)

    @pl.when(j == pl.num_programs(1) - 1)
    def _():
        for r in range(0, o_ref.shape[0], NORM_CHUNK_ROWS):
            rows = slice(r, r + NORM_CHUNK_ROWS)
            o_ref[rows, :] = h_ref[rows, :] + _rms(o_ref[rows, :], post_g_ref[...])


def _ffn(h, pre_g, post_g, w_in, w_out, layer, *, tm=1024, tf=512):
    m, d = h.shape
    f = w_out.shape[1]
    tm = _row_tile(m, tm)
    tf = _row_tile(f, tf)
    nf = f // tf
    return pl.pallas_call(
        _ffn_kernel,
        grid=(m // tm, nf),
        in_specs=[
            pl.BlockSpec((tm, d), lambda i, j: (i, 0)),
            pl.BlockSpec((1, d), lambda i, j: (0, 0)),
            pl.BlockSpec((1, d), lambda i, j: (0, 0)),
            pl.BlockSpec((None, d, tf), lambda i, j: (layer, 0, j)),
            pl.BlockSpec((None, d, tf), lambda i, j: (layer, 0, j + nf)),
            pl.BlockSpec((None, tf, d), lambda i, j: (layer, j, 0)),
        ],
        out_specs=pl.BlockSpec((tm, d), lambda i, j: (i, 0)),
        out_shape=jax.ShapeDtypeStruct((m, d), F32),
        scratch_shapes=[pltpu.VMEM((tm, d), BF16)],
        compiler_params=_params("arbitrary", "arbitrary", vmem_limit_bytes=FFN_VMEM_LIMIT_BYTES),
        name="ffn",
    )(h, pre_g.reshape(1, d), post_g.reshape(1, d), w_in, w_in, w_out)


LRU_PAD = SUBLANES
LRU_INTERLEAVE = 3


def _lru_steps(first, count):
    return pl.ds(2 * (LRU_PAD + first), count, stride=2)


def _rglru_reset(xs_ref, hprev_ref, ts):
    hprev_ref[...] = jnp.zeros_like(hprev_ref)
    for n in range(xs_ref.shape[0]):
        xs_ref[n, _lru_steps(ts - LRU_PAD, LRU_PAD), :] = jnp.zeros((LRU_PAD, HEAD_DIM), F32)


def _rglru_tile(x_of, gate_of, y_ref, cw_ref, cb_ref, wa_ref, ba_ref, wx_ref, bx_ref, lam_ref,
                xs_ref, as_ref, bs_ref, hprev_ref, ts):
    pad, steps = LRU_PAD, _lru_steps
    n_blocks = cw_ref.shape[1] // HEAD_DIM

    for n in range(n_blocks):
        xs_ref[n, steps(-pad, pad), :] = xs_ref[n, steps(ts - pad, pad), :]
        as_ref[n, steps(-pad, pad), :] = jnp.ones((pad, HEAD_DIM), F32)
        bs_ref[n, steps(-pad, pad), :] = jnp.zeros((pad, HEAD_DIM), F32)

    def channel_block(n):
        sl = slice(n * HEAD_DIM, (n + 1) * HEAD_DIM)
        x = x_of(n).astype(F32)
        xs_ref[n, steps(0, ts), :] = x
        yield
        cw = cw_ref[:, sl]
        xc = cb_ref[:, sl] + cw[CONV_WIDTH - 1:CONV_WIDTH, :] * x
        for s in range(1, CONV_WIDTH):
            xc = xc + cw[CONV_WIDTH - 1 - s:CONV_WIDTH - s, :] * xs_ref[n, steps(-s, ts), :]

        xcb = xc.astype(BF16)
        r = jax.nn.sigmoid(jnp.dot(xcb, wa_ref[n], preferred_element_type=F32) + ba_ref[:, sl])
        i = jax.nn.sigmoid(jnp.dot(xcb, wx_ref[n], preferred_element_type=F32) + bx_ref[:, sl])
        neg_lam = -lam_ref[:, sl]
        softplus = jnp.maximum(neg_lam, 0.0) + jnp.log1p(jnp.exp(-jnp.abs(neg_lam)))
        a = jnp.exp2(r * ((-LRU_C * math.log2(math.e)) * softplus))
        z = 1.0 - a * a
        b = jnp.where(z > 0.0, z * lax.rsqrt(z), 0.0) * (i * xc)

        for shift in (1, 2, 4):
            as_ref[n, steps(0, ts), :] = a
            bs_ref[n, steps(0, ts), :] = b
            yield
            b = a * bs_ref[n, steps(-shift, ts), :] + b
            a = a * as_ref[n, steps(-shift, ts), :]
        hg = a[:SUBLANES] * hprev_ref[:, sl] + b[:SUBLANES]
        groups = [hg]
        for v in range(1, ts // SUBLANES):
            rows = slice(v * SUBLANES, (v + 1) * SUBLANES)
            hg = a[rows] * hg + b[rows]
            groups.append(hg)
        h = jnp.concatenate(groups, axis=0)
        hprev_ref[:, sl] = hg[SUBLANES - 1:, :]

        gb = gate_of(n).astype(F32)
        k = math.sqrt(2.0 / math.pi)
        cdf = 0.5 * (1.0 + jnp.tanh(gb * (k + (k * 0.044715) * (gb * gb))))
        y_ref[:, sl] = (h * (gb * cdf)).astype(y_ref.dtype)

    for first in range(0, n_blocks, LRU_INTERLEAVE):
        running = [channel_block(n) for n in range(first, min(first + LRU_INTERLEAVE, n_blocks))]
        while running:
            running = [blk for blk in running if next(blk, StopIteration) is not StopIteration]


def _rglru_kernel(xb_ref, gb_ref, cw_ref, cb_ref, wa_ref, ba_ref, wx_ref, bx_ref, lam_ref,
                  *refs, n_cast):
    cast_in, y_ref, cast_out = refs[:n_cast], refs[n_cast], refs[n_cast + 1:2 * n_cast + 1]
    xs_ref, as_ref, bs_ref, hprev_ref = refs[2 * n_cast + 1:]
    ts = xb_ref.shape[0]
    _side_cast(cast_in, cast_out)

    @pl.when(pl.program_id(1) == 0)
    def _():
        _rglru_reset(xs_ref, hprev_ref, ts)

    _rglru_tile(lambda n: xb_ref[:, n * HEAD_DIM:(n + 1) * HEAD_DIM],
                lambda n: gb_ref[:, n * HEAD_DIM:(n + 1) * HEAD_DIM],
                y_ref, cw_ref, cb_ref, wa_ref, ba_ref, wx_ref, bx_ref, lam_ref,
                xs_ref, as_ref, bs_ref, hprev_ref, ts)


def _rglru(proj, conv_w, conv_b, wa, ba, wx, bx, lam, layer, casts=(), *, ts=512):
    bsz, s, _ = proj.shape
    c = conv_w.shape[1]
    ts = _row_tile(s, ts)
    nt = s // ts
    vec = lambda v: v.reshape(1, c)
    full2 = lambda shape: pl.BlockSpec(shape, lambda b, i: (0, 0))
    gate_w = lambda w: pl.BlockSpec((None,) + w.shape[1:], lambda b, i: (layer, 0, 0, 0))
    stage = pltpu.VMEM((c // HEAD_DIM, 2 * (ts + LRU_PAD), HEAD_DIM), F32)
    cast_in, cast_out, cast_shapes = _side_cast_specs(casts, bsz * nt, lambda b, i: b * nt + i)
    out = pl.pallas_call(
        functools.partial(_rglru_kernel, n_cast=len(casts)),
        grid=(bsz, nt),
        in_specs=[
            pl.BlockSpec((None, ts, c), lambda b, i: (b, i, 0)),
            pl.BlockSpec((None, ts, c), lambda b, i: (b, i, 1)),
            full2((CONV_WIDTH, c)), full2((1, c)),
            gate_w(wa), full2((1, c)),
            gate_w(wx), full2((1, c)),
            full2((1, c)),
        ] + cast_in,
        out_specs=[pl.BlockSpec((None, ts, c), lambda b, i: (b, i, 0))] + cast_out,
        out_shape=[jax.ShapeDtypeStruct((bsz, s, c), BF16)] + cast_shapes,
        scratch_shapes=[stage, stage, stage, pltpu.VMEM((1, c), F32)],
        compiler_params=_params("arbitrary", "arbitrary"),
        name="rglru",
    )(proj, proj, conv_w, vec(conv_b), wa, vec(ba), wx, vec(bx), vec(lam),
      *[cw for cw, _ in casts])
    return out[0], out[1:]


def _band_masks(first_block, dil):
    qi = lax.broadcasted_iota(jnp.int32, (Q_BLOCK, 2 * Q_BLOCK), 0)
    ki = lax.broadcasted_iota(jnp.int32, (Q_BLOCK, 2 * Q_BLOCK), 1)
    rel = qi + Q_BLOCK - ki
    valid = (rel >= 0) & (rel <= Q_BLOCK)
    valid_first = valid & (ki >= jnp.where(first_block, Q_BLOCK, 0))
    return valid, valid_first, (rel * dil).astype(F32)


def _alibi_slope(group, hh):
    n_heads = len(DIL_GROUPS) * HEADS_PER_GROUP
    return 2.0 ** (-8.0 * (group * HEADS_PER_GROUP + hh + 1) / n_heads)


def _attend(q, k, v, bias, mask):
    sc = lax.dot_general(q, k, (((1,), (1,)), ((), ())),
                         preferred_element_type=F32) * (HEAD_DIM ** -0.5)
    sc = jnp.where(mask, sc + bias, NEG_INF)
    mx = jnp.max(sc, axis=-1, keepdims=True)
    e = jnp.exp(sc - mx)
    den = jnp.sum(e, axis=-1, keepdims=True)
    out = jnp.dot(e.astype(BF16), v, preferred_element_type=F32) / den
    return out, mx + jnp.log(den)


def _dil_attn_dense_kernel(q_ref, kp_ref, kc_ref, vp_ref, vc_ref, o_ref, l_ref, *, group):
    valid, valid_first, dist = _band_masks(pl.program_id(1) == 0, 1)
    for hh in range(HEADS_PER_GROUP):
        hs = slice(hh * HEAD_DIM, (hh + 1) * HEAD_DIM)
        bias = -_alibi_slope(group, hh) * dist
        for j in range(q_ref.shape[0] // Q_BLOCK):
            cur = slice(j * Q_BLOCK, (j + 1) * Q_BLOCK)
            if j == 0:
                k_prev, v_prev, mask = kp_ref[:, hs], vp_ref[:, hs], valid_first
            else:
                prev = slice((j - 1) * Q_BLOCK, j * Q_BLOCK)
                k_prev, v_prev, mask = kc_ref[prev, hs], vc_ref[prev, hs], valid
            k = jnp.concatenate([k_prev, kc_ref[cur, hs]], axis=0)
            v = jnp.concatenate([v_prev, vc_ref[cur, hs]], axis=0)
            out, lse = _attend(q_ref[cur, hs], k, v, bias, mask)
            o_ref[cur, hs] = out.astype(o_ref.dtype)
            l_ref[cur, hs] = jnp.broadcast_to(lse, (Q_BLOCK, HEAD_DIM))


RESIDUE_RADIX = 4


class _ResidueStage:
    def __init__(self, stage_ref, mid_ref, dil):
        self.stage, self.mid, self.dil = stage_ref, mid_ref, dil
        self.two_hops = dil > RESIDUE_RADIX
        assert dil == RESIDUE_RADIX or dil == RESIDUE_RADIX ** 2

    def _hop_rows(self, low):
        return pl.ds(low, self.stage.shape[0] // RESIDUE_RADIX, stride=RESIDUE_RADIX)

    def load(self, value):
        self.stage[...] = value
        if self.two_hops:
            for low in range(RESIDUE_RADIX):
                self.mid[low] = self.stage[self._hop_rows(low), :]

    def _mid_rows(self, r):
        return pl.ds(r // RESIDUE_RADIX, Q_BLOCK, stride=self.dil // RESIDUE_RADIX)

    def read(self, r):
        if self.two_hops:
            return self.mid[r % RESIDUE_RADIX, self._mid_rows(r), :]
        return self.stage[pl.ds(r, Q_BLOCK, stride=self.dil), :]

    def write(self, r, value):
        if self.two_hops:
            self.mid[r % RESIDUE_RADIX, self._mid_rows(r), :] = value
        else:
            self.stage[pl.ds(r, Q_BLOCK, stride=self.dil), :] = value

    def store(self):
        if self.two_hops:
            for low in range(RESIDUE_RADIX):
                self.stage[self._hop_rows(low), :] = self.mid[low]
        return self.stage[...]


def _dil_attn_strided_kernel(q_ref, k_ref, v_ref, o_ref, l_ref, kd_ref, vd_ref, *stage_refs,
                             dil, group):
    first_span = pl.program_id(1) == 0

    @pl.when(first_span)
    def _():
        kd_ref[...] = jnp.zeros_like(kd_ref)
        vd_ref[...] = jnp.zeros_like(vd_ref)

    refs = list(stage_refs)
    qs, ks, vs, os_, ls = (
        _ResidueStage(refs.pop(0), refs.pop(0) if dil > RESIDUE_RADIX else None, dil)
        for _ in range(5))
    _, valid_first, dist = _band_masks(first_span, dil)
    for hh in range(HEADS_PER_GROUP):
        hs = slice(hh * HEAD_DIM, (hh + 1) * HEAD_DIM)
        bias = -_alibi_slope(group, hh) * dist
        qs.load(q_ref[:, hs].astype(F32))
        ks.load(k_ref[:, hs].astype(F32))
        vs.load(v_ref[:, hs].astype(F32))

        for r in range(dil):
            k_cur = ks.read(r).astype(BF16)
            v_cur = vs.read(r).astype(BF16)
            k = jnp.concatenate([kd_ref[hh, r], k_cur], axis=0)
            v = jnp.concatenate([vd_ref[hh, r], v_cur], axis=0)
            out, lse = _attend(qs.read(r).astype(BF16), k, v, bias, valid_first)
            kd_ref[hh, r] = k_cur
            vd_ref[hh, r] = v_cur
            os_.write(r, out)
            ls.write(r, jnp.broadcast_to(lse, (Q_BLOCK, HEAD_DIM)))
        o_ref[:, hs] = os_.store().astype(o_ref.dtype)
        l_ref[:, hs] = ls.store()


def _dil_attn(proj, kv, group, *, dense_rows=1024):
    _, dil = DIL_GROUPS[group]
    bsz, s, _ = proj.shape
    v_off = kv.shape[2] // (2 * GROUP_WIDTH)
    assert s % (dil * Q_BLOCK) == 0
    if dil == 1:
        rows = _row_tile(s, dense_rows)
        ratio = rows // Q_BLOCK
        prev = lambda col: pl.BlockSpec((None, Q_BLOCK, GROUP_WIDTH),
                                        lambda b, n: (b, jnp.maximum(n * ratio - 1, 0), col))
        body = functools.partial(_dil_attn_dense_kernel, group=group)
        operands, scratch = ("q", "kp", "k", "vp", "v"), []
    else:
        rows = dil * Q_BLOCK
        body = functools.partial(_dil_attn_strided_kernel, dil=dil, group=group)
        operands = ("q", "k", "v")
        split = pltpu.VMEM((HEADS_PER_GROUP, dil, Q_BLOCK, HEAD_DIM), BF16)
        stage = [pltpu.VMEM((rows, HEAD_DIM), F32)]
        if dil > RESIDUE_RADIX:
            stage.append(pltpu.VMEM((RESIDUE_RADIX, rows // RESIDUE_RADIX, HEAD_DIM), F32))
        scratch = [split, split] + stage * 5
    cur = lambda col: pl.BlockSpec((None, rows, GROUP_WIDTH), lambda b, n: (b, n, col))
    specs = {"q": cur(group), "k": cur(group), "v": cur(v_off + group)}
    if dil == 1:
        specs.update(kp=prev(group), vp=prev(v_off + group))
    return pl.pallas_call(
        body,
        grid=(bsz, s // rows),
        in_specs=[specs[name] for name in operands],
        out_specs=[cur(0), cur(0)],
        out_shape=[jax.ShapeDtypeStruct((bsz, s, GROUP_WIDTH), BF16),
                   jax.ShapeDtypeStruct((bsz, s, GROUP_WIDTH), F32)],
        scratch_shapes=scratch,
        compiler_params=_params("arbitrary", "arbitrary"),
        name=f"dil_attn_g{group}",
    )(*[proj if name == "q" else kv for name in operands])


def _mem_attn(q, kv_ref):
    outs = []
    for hh in range(MEM_HEADS):
        hs = slice(hh * HEAD_DIM, (hh + 1) * HEAD_DIM)
        vs = slice(MEM_WIDTH + hh * HEAD_DIM, MEM_WIDTH + (hh + 1) * HEAD_DIM)
        sc = lax.dot_general(q[:, hs], kv_ref[:, hs], (((1,), (1,)), ((), ())),
                             preferred_element_type=F32) * (HEAD_DIM ** -0.5)
        e = jnp.exp(sc - jnp.max(sc, axis=-1, keepdims=True))
        den = jnp.sum(e, axis=-1, keepdims=True)
        outs.append(jnp.dot(e.astype(BF16), kv_ref[:, vs], preferred_element_type=F32) / den)
    return jnp.concatenate(outs, axis=-1)


def _mix_out_tail(mixer, q_ref, kv_ref, w_ref, g_ref, h_ref, o_ref):
    m = _mem_attn(q_ref[...], kv_ref)
    mix = jnp.concatenate([mixer, m.astype(BF16)], axis=-1)
    out = jnp.dot(mix, w_ref[...], preferred_element_type=F32)
    o_ref[...] = h_ref[...] + _rms(out, g_ref[...])


def _mix_out_rec_kernel(y_ref, q_ref, kv_ref, w_ref, g_ref, h_ref, o_ref):
    _mix_out_tail(y_ref[...], q_ref, kv_ref, w_ref, g_ref, h_ref, o_ref)


def _mix_out_dil_kernel(o0_ref, o1_ref, o2_ref, l0_ref, l1_ref, l2_ref,
                        q_ref, kv_ref, w_ref, g_ref, h_ref, o_ref):
    lses = [l0_ref[...], l1_ref[...], l2_ref[...]]
    mx = jnp.maximum(jnp.maximum(lses[0], lses[1]), lses[2])
    es = [jnp.exp(l - mx) for l in lses]
    den = es[0] + es[1] + es[2]
    parts = [(o_ref_g[...].astype(F32) * (e / den)).astype(BF16)
             for o_ref_g, e in zip((o0_ref, o1_ref, o2_ref), es)]
    _mix_out_tail(jnp.concatenate(parts, axis=-1), q_ref, kv_ref, w_ref, g_ref, h_ref, o_ref)


def _mix_out(mixer_inputs, proj, memkv, w_out, post_g, h, layer, *, tm=512):
    bsz, s, d = h.shape
    tm = _row_tile(s, tm)
    q_block = proj.shape[2] // MEM_WIDTH - 1
    row_spec = lambda width, col=0: pl.BlockSpec((None, tm, width), lambda b, i: (b, i, col))
    kernel = _mix_out_rec_kernel if len(mixer_inputs) == 1 else _mix_out_dil_kernel
    return pl.pallas_call(
        kernel,
        grid=(bsz, s // tm),
        in_specs=[row_spec(a.shape[2]) for a in mixer_inputs] + [
            row_spec(MEM_WIDTH, q_block),
            pl.BlockSpec((None,) + memkv.shape[1:], lambda b, i: (b, 0, 0)),
            _resident((None,) + w_out.shape[1:], lambda b, i: (layer, 0, 0)),
            _resident((1, d), lambda b, i: (0, 0)),
            row_spec(d),
        ],
        out_specs=row_spec(d),
        out_shape=jax.ShapeDtypeStruct((bsz, s, d), F32),
        compiler_params=_params("arbitrary", "arbitrary"),
        name="mix_out_rec" if len(mixer_inputs) == 1 else "mix_out_dil",
    )(*mixer_inputs, proj, memkv, w_out, post_g.reshape(1, d), h)


def kernel(x, mem, mem_norm_g, a_pre_mix_g, a_post_mix_g, a_pre_ffn_g, a_post_ffn_g, a_w_in, a_conv_w, a_conv_b, a_gate_a_w, a_gate_a_b, a_gate_x_w, a_gate_x_b, a_lambda, a_w_mem_kv, a_w_out, a_w_ffn_in, a_w_ffn_out, kv_norm_g, w_kv_shared, b_pre_mix_g, b_post_mix_g, b_pre_ffn_g, b_post_ffn_g, b_w_in, b_w_mem_kv, b_w_out, b_w_ffn_in, b_w_ffn_out):
    bsz, s, d = x.shape
    n_mem = mem.shape[1]
    n_a = a_w_in.shape[0]
    n_b = b_w_in.shape[0]
    bf = lambda w: w.astype(BF16)
    a_w_in, a_gate_a_w, a_gate_x_w, a_w_mem_kv, b_w_mem_kv = map(
        bf, (a_w_in, a_gate_a_w, a_gate_x_w, a_w_mem_kv, b_w_mem_kv))

    def flat(t):
        return t.reshape(bsz * s, t.shape[-1])

    def mem_kv(w, l):
        kv_mem = _norm_matmul(mem.reshape(bsz * n_mem, d), mem_norm_g, w, l)
        return kv_mem.reshape(bsz, n_mem, -1)

    def ffn(h, pre_g, post_g, w_in, w_out):
        return _ffn(flat(h), pre_g, post_g, w_in, w_out, 0).reshape(bsz, s, d)

    h = x
    for l in range(n_a):
        proj = _norm_matmul(flat(h), a_pre_mix_g[l], a_w_in, l).reshape(bsz, s, -1)
        casts = [(a_w_ffn_in, l), (a_w_ffn_out, l), (a_w_out, l)]
        if l == n_a - 1:
            casts += [(w_kv_shared[None], 0), (b_w_in.reshape(1, n_b * d, -1), 0)]
        y, cast = _rglru(proj, a_conv_w[l], a_conv_b[l], a_gate_a_w, a_gate_a_b[l],
                         a_gate_x_w, a_gate_x_b[l], a_lambda[l], l, casts)
        h = _mix_out([y], proj, mem_kv(a_w_mem_kv, l), cast[2], a_post_mix_g[l], h, 0)
        h = ffn(h, a_pre_ffn_g[l], a_post_ffn_g[l], cast[0], cast[1])
    w_kv_bf, b_w_in_bf = cast[3], cast[4].reshape(b_w_in.shape)

    kv = _norm_matmul(flat(h), kv_norm_g, w_kv_bf, 0).reshape(bsz, s, -1)
    for l in range(n_b):
        proj, cast = _norm_matmul(flat(h), b_pre_mix_g[l], b_w_in_bf, l,
                                  [(b_w_ffn_in, l), (b_w_ffn_out, l), (b_w_out, l)])
        proj = proj.reshape(bsz, s, -1)
        outs, lses = zip(*[_dil_attn(proj, kv, g) for g in range(len(DIL_GROUPS))])
        h = _mix_out(list(outs) + list(lses), proj, mem_kv(b_w_mem_kv, l), cast[2],
                     b_post_mix_g[l], h, 0)
        h = ffn(h, b_pre_ffn_g[l], b_post_ffn_g[l], cast[0], cast[1])
    return h
```

```python
import functools
import math

import jax
import jax.numpy as jnp
from jax import lax
from jax.experimental import pallas as pl
from jax.experimental.pallas import tpu as pltpu

HEAD_DIM = 128
MEM_HEADS = 4
MEM_WIDTH = MEM_HEADS * HEAD_DIM
CONV_WIDTH = 4
LRU_C = 8.0
DIL_GROUPS = ((128, 1), (512, 4), (2048, 16))
HEADS_PER_GROUP = 4
GROUP_WIDTH = HEADS_PER_GROUP * HEAD_DIM
Q_BLOCK = 128
RMS_EPS = 1e-6
NEG_INF = -1e30

SUBLANES = 8
NORM_CHUNK_ROWS = 2 * SUBLANES
V7X_VMEM_BYTES = 64 * 1024 * 1024
VMEM_LIMIT_BYTES = V7X_VMEM_BYTES - 8 * 1024 * 1024
FFN_VMEM_LIMIT_BYTES = V7X_VMEM_BYTES - 2 * 1024 * 1024

F32 = jnp.float32
BF16 = jnp.bfloat16


def _params(*semantics, vmem_limit_bytes=VMEM_LIMIT_BYTES):
    return pltpu.CompilerParams(dimension_semantics=semantics, vmem_limit_bytes=vmem_limit_bytes)


def _rms(x, g):
    return x * lax.rsqrt(jnp.mean(x * x, axis=-1, keepdims=True) + RMS_EPS) * g


def _row_tile(rows, target):
    t = min(rows, target)
    assert rows % t == 0, (rows, t)
    return t


BF16_ROW_TILE = 16


def _side_cast_specs(casts, n_steps, step_of):
    in_specs, out_specs, out_shapes = [], [], []
    for w, layer in casts:
        _, rows, cols = w.shape
        slab = rows // n_steps
        assert rows % n_steps == 0 and slab % BF16_ROW_TILE == 0, (w.shape, n_steps)
        in_specs.append(pl.BlockSpec(
            (None, slab, cols), lambda *ids, layer=layer: (layer, step_of(*ids), 0)))
        out_specs.append(pl.BlockSpec((None, slab, cols), lambda *ids: (0, step_of(*ids), 0)))
        out_shapes.append(jax.ShapeDtypeStruct((1, rows, cols), BF16))
    return in_specs, out_specs, out_shapes


def _side_cast(src_refs, dst_refs):
    for src, dst in zip(src_refs, dst_refs):
        dst[...] = src[...].astype(dst.dtype)


def _norm_matmul_kernel(h_ref, g_ref, w_ref, *refs, tn):
    n_cast = len(refs) // 2
    o_ref = refs[n_cast]
    _side_cast(refs[:n_cast], refs[n_cast + 1:])
    hn = _rms(h_ref[...], g_ref[...]).astype(BF16)
    for c in range(0, o_ref.shape[1], tn):
        o_ref[:, c:c + tn] = jnp.dot(hn, w_ref[:, c:c + tn],
                                     preferred_element_type=F32).astype(o_ref.dtype)


def _resident(block_shape, index_map):
    return pl.BlockSpec(block_shape, index_map, pipeline_mode=pl.Buffered(1))


def _norm_matmul(h, g, w, layer, casts=(), *, tm=512, tn=512):
    m, d = h.shape
    n = w.shape[2]
    tm = _row_tile(m, tm)
    tn = _row_tile(n, tn)
    cast_in, cast_out, cast_shapes = _side_cast_specs(casts, m // tm, lambda i: i)
    out = pl.pallas_call(
        functools.partial(_norm_matmul_kernel, tn=tn),
        grid=(m // tm,),
        in_specs=[
            pl.BlockSpec((tm, d), lambda i: (i, 0)),
            _resident((1, d), lambda i: (0, 0)),
            _resident((None, d, n), lambda i: (layer, 0, 0)),
        ] + cast_in,
        out_specs=[pl.BlockSpec((tm, n), lambda i: (i, 0))] + cast_out,
        out_shape=[jax.ShapeDtypeStruct((m, n), BF16)] + cast_shapes,
        compiler_params=_params("arbitrary"),
        name="norm_matmul",
    )(h, g.reshape(1, d), w, *[cw for cw, _ in casts])
    return (out[0], out[1:]) if casts else out[0]


def _ffn_kernel(h_ref, pre_g_ref, post_g_ref, wg_ref, wu_ref, wo_ref, o_ref, hn_ref):
    j = pl.program_id(1)

    @pl.when(j == 0)
    def _():
        hn_ref[...] = _rms(h_ref[...], pre_g_ref[...]).astype(hn_ref.dtype)
        o_ref[...] = jnp.zeros_like(o_ref)

    hn = hn_ref[...]
    gate = jnp.dot(hn, wg_ref[...], preferred_element_type=F32)
    up = jnp.dot(hn, wu_ref[...], preferred_element_type=F32)
    act = (gate * jax.nn.sigmoid(gate) * up).astype(BF16)
    o_ref[...] += jnp.dot(act, wo_ref[...], preferred_element_type=F32)

    @pl.when(j == pl.num_programs(1) - 1)
    def _():
        for r in range(0, o_ref.shape[0], NORM_CHUNK_ROWS):
            rows = slice(r, r + NORM_CHUNK_ROWS)
            o_ref[rows, :] = h_ref[rows, :] + _rms(o_ref[rows, :], post_g_ref[...])


def _ffn(h, pre_g, post_g, w_in, w_out, layer, *, tm=1024, tf=512):
    m, d = h.shape
    f = w_out.shape[1]
    tm = _row_tile(m, tm)
    tf = _row_tile(f, tf)
    nf = f // tf
    return pl.pallas_call(
        _ffn_kernel,
        grid=(m // tm, nf),
        in_specs=[
            pl.BlockSpec((tm, d), lambda i, j: (i, 0)),
            pl.BlockSpec((1, d), lambda i, j: (0, 0)),
            pl.BlockSpec((1, d), lambda i, j: (0, 0)),
            pl.BlockSpec((None, d, tf), lambda i, j: (layer, 0, j)),
            pl.BlockSpec((None, d, tf), lambda i, j: (layer, 0, j + nf)),
            pl.BlockSpec((None, tf, d), lambda i, j: (layer, j, 0)),
        ],
        out_specs=pl.BlockSpec((tm, d), lambda i, j: (i, 0)),
        out_shape=jax.ShapeDtypeStruct((m, d), F32),
        scratch_shapes=[pltpu.VMEM((tm, d), BF16)],
        compiler_params=_params("arbitrary", "arbitrary", vmem_limit_bytes=FFN_VMEM_LIMIT_BYTES),
        name="ffn",
    )(h, pre_g.reshape(1, d), post_g.reshape(1, d), w_in, w_in, w_out)


LRU_PAD = SUBLANES


def _lru_steps(first, count):
    return pl.ds(LRU_PAD + first, count, stride=1)


def _rglru_reset(xs_ref, hprev_ref, ts):
    hprev_ref[...] = jnp.zeros_like(hprev_ref)
    for n in range(xs_ref.shape[0]):
        xs_ref[n, _lru_steps(ts - LRU_PAD, LRU_PAD), :] = jnp.zeros((LRU_PAD, HEAD_DIM), F32)


def _rglru_tile(x_of, gate_of, y_ref, cw_ref, cb_ref, wa_ref, ba_ref, wx_ref, bx_ref, lam_ref,
                xs_ref, as_ref, bs_ref, hprev_ref, ts):
    pad, steps = LRU_PAD, _lru_steps
    n_blocks = cw_ref.shape[1] // HEAD_DIM

    for n in range(n_blocks):
        xs_ref[n, steps(-pad, pad), :] = xs_ref[n, steps(ts - pad, pad), :]
        as_ref[n, steps(-pad, pad), :] = jnp.ones((pad, HEAD_DIM), F32)
        bs_ref[n, steps(-pad, pad), :] = jnp.zeros((pad, HEAD_DIM), F32)

    for n in range(n_blocks):
        sl = slice(n * HEAD_DIM, (n + 1) * HEAD_DIM)
        x = x_of(n).astype(F32)
        xs_ref[n, steps(0, ts), :] = x
        cw = cw_ref[:, sl]
        xc = cb_ref[:, sl] + cw[CONV_WIDTH - 1:CONV_WIDTH, :] * x
        for s in range(1, CONV_WIDTH):
            xc = xc + cw[CONV_WIDTH - 1 - s:CONV_WIDTH - s, :] * xs_ref[n, steps(-s, ts), :]

        xcb = xc.astype(BF16)
        r = jax.nn.sigmoid(jnp.dot(xcb, wa_ref[n], preferred_element_type=F32) + ba_ref[:, sl])
        i = jax.nn.sigmoid(jnp.dot(xcb, wx_ref[n], preferred_element_type=F32) + bx_ref[:, sl])
        neg_lam = -lam_ref[:, sl]
        softplus = jnp.maximum(neg_lam, 0.0) + jnp.log1p(jnp.exp(-jnp.abs(neg_lam)))
        a = jnp.exp2(r * ((-LRU_C * math.log2(math.e)) * softplus))
        z = 1.0 - a * a
        b = jnp.where(z > 0.0, z * lax.rsqrt(z), 0.0) * (i * xc)

        for shift in (1, 2, 4):
            as_ref[n, steps(0, ts), :] = a
            bs_ref[n, steps(0, ts), :] = b
            b = a * bs_ref[n, steps(-shift, ts), :] + b
            a = a * as_ref[n, steps(-shift, ts), :]
        hg = a[:SUBLANES] * hprev_ref[:, sl] + b[:SUBLANES]
        groups = [hg]
        for v in range(1, ts // SUBLANES):
            rows = slice(v * SUBLANES, (v + 1) * SUBLANES)
            hg = a[rows] * hg + b[rows]
            groups.append(hg)
        h = jnp.concatenate(groups, axis=0)
        hprev_ref[:, sl] = hg[SUBLANES - 1:, :]

        gb = gate_of(n).astype(F32)
        k = math.sqrt(2.0 / math.pi)
        cdf = 0.5 * (1.0 + jnp.tanh(gb * (k + (k * 0.044715) * (gb * gb))))
        y_ref[:, sl] = (h * (gb * cdf)).astype(y_ref.dtype)


def _rglru_kernel(xb_ref, gb_ref, cw_ref, cb_ref, wa_ref, ba_ref, wx_ref, bx_ref, lam_ref,
                  *refs, n_cast):
    cast_in, y_ref, cast_out = refs[:n_cast], refs[n_cast], refs[n_cast + 1:2 * n_cast + 1]
    xs_ref, as_ref, bs_ref, hprev_ref = refs[2 * n_cast + 1:]
    ts = xb_ref.shape[0]
    _side_cast(cast_in, cast_out)

    @pl.when(pl.program_id(1) == 0)
    def _():
        _rglru_reset(xs_ref, hprev_ref, ts)

    _rglru_tile(lambda n: xb_ref[:, n * HEAD_DIM:(n + 1) * HEAD_DIM],
                lambda n: gb_ref[:, n * HEAD_DIM:(n + 1) * HEAD_DIM],
                y_ref, cw_ref, cb_ref, wa_ref, ba_ref, wx_ref, bx_ref, lam_ref,
                xs_ref, as_ref, bs_ref, hprev_ref, ts)


def _rglru(proj, conv_w, conv_b, wa, ba, wx, bx, lam, layer, casts=(), *, ts=512):
    bsz, s, _ = proj.shape
    c = conv_w.shape[1]
    ts = _row_tile(s, ts)
    nt = s // ts
    vec = lambda v: v.reshape(1, c)
    full2 = lambda shape: pl.BlockSpec(shape, lambda b, i: (0, 0))
    gate_w = lambda w: pl.BlockSpec((None,) + w.shape[1:], lambda b, i: (layer, 0, 0, 0))
    stage = pltpu.VMEM((c // HEAD_DIM, ts + LRU_PAD, HEAD_DIM), F32)
    cast_in, cast_out, cast_shapes = _side_cast_specs(casts, bsz * nt, lambda b, i: b * nt + i)
    out = pl.pallas_call(
        functools.partial(_rglru_kernel, n_cast=len(casts)),
        grid=(bsz, nt),
        in_specs=[
            pl.BlockSpec((None, ts, c), lambda b, i: (b, i, 0)),
            pl.BlockSpec((None, ts, c), lambda b, i: (b, i, 1)),
            full2((CONV_WIDTH, c)), full2((1, c)),
            gate_w(wa), full2((1, c)),
            gate_w(wx), full2((1, c)),
            full2((1, c)),
        ] + cast_in,
        out_specs=[pl.BlockSpec((None, ts, c), lambda b, i: (b, i, 0))] + cast_out,
        out_shape=[jax.ShapeDtypeStruct((bsz, s, c), BF16)] + cast_shapes,
        scratch_shapes=[stage, stage, stage, pltpu.VMEM((1, c), F32)],
        compiler_params=_params("arbitrary", "arbitrary"),
        name="rglru",
    )(proj, proj, conv_w, vec(conv_b), wa, vec(ba), wx, vec(bx), vec(lam),
      *[cw for cw, _ in casts])
    return out[0], out[1:]


def _band_masks(first_block, dil):
    qi = lax.broadcasted_iota(jnp.int32, (Q_BLOCK, 2 * Q_BLOCK), 0)
    ki = lax.broadcasted_iota(jnp.int32, (Q_BLOCK, 2 * Q_BLOCK), 1)
    rel = qi + Q_BLOCK - ki
    valid = (rel >= 0) & (rel <= Q_BLOCK)
    valid_first = valid & (ki >= jnp.where(first_block, Q_BLOCK, 0))
    return valid, valid_first, (rel * dil).astype(F32)


def _alibi_slope(group, hh):
    n_heads = len(DIL_GROUPS) * HEADS_PER_GROUP
    return 2.0 ** (-8.0 * (group * HEADS_PER_GROUP + hh + 1) / n_heads)


def _attend(q, k, v, bias, mask):
    sc = lax.dot_general(q, k, (((1,), (1,)), ((), ())),
                         preferred_element_type=F32) * (HEAD_DIM ** -0.5)
    sc = jnp.where(mask, sc + bias, NEG_INF)
    mx = jnp.max(sc, axis=-1, keepdims=True)
    e = jnp.exp(sc - mx)
    den = jnp.sum(e, axis=-1, keepdims=True)
    out = jnp.dot(e.astype(BF16), v, preferred_element_type=F32) / den
    return out, mx + jnp.log(den)


def _dil_attn_dense_kernel(q_ref, kp_ref, kc_ref, vp_ref, vc_ref, o_ref, l_ref, *, group):
    valid, valid_first, dist = _band_masks(pl.program_id(1) == 0, 1)
    for hh in range(HEADS_PER_GROUP):
        hs = slice(hh * HEAD_DIM, (hh + 1) * HEAD_DIM)
        bias = -_alibi_slope(group, hh) * dist
        for j in range(q_ref.shape[0] // Q_BLOCK):
            cur = slice(j * Q_BLOCK, (j + 1) * Q_BLOCK)
            if j == 0:
                k_prev, v_prev, mask = kp_ref[:, hs], vp_ref[:, hs], valid_first
            else:
                prev = slice((j - 1) * Q_BLOCK, j * Q_BLOCK)
                k_prev, v_prev, mask = kc_ref[prev, hs], vc_ref[prev, hs], valid
            k = jnp.concatenate([k_prev, kc_ref[cur, hs]], axis=0)
            v = jnp.concatenate([v_prev, vc_ref[cur, hs]], axis=0)
            out, lse = _attend(q_ref[cur, hs], k, v, bias, mask)
            o_ref[cur, hs] = out.astype(o_ref.dtype)
            l_ref[cur, hs] = jnp.broadcast_to(lse, (Q_BLOCK, HEAD_DIM))


RESIDUE_RADIX = 4


class _ResidueStage:
    def __init__(self, stage_ref, mid_ref, dil):
        self.stage, self.mid, self.dil = stage_ref, mid_ref, dil
        self.two_hops = dil > RESIDUE_RADIX
        assert dil == RESIDUE_RADIX or dil == RESIDUE_RADIX ** 2

    def _hop_rows(self, low):
        return pl.ds(low, self.stage.shape[0] // RESIDUE_RADIX, stride=RESIDUE_RADIX)

    def load(self, value):
        self.stage[...] = value
        if self.two_hops:
            for low in range(RESIDUE_RADIX):
                self.mid[low] = self.stage[self._hop_rows(low), :]

    def _mid_rows(self, r):
        return pl.ds(r // RESIDUE_RADIX, Q_BLOCK, stride=self.dil // RESIDUE_RADIX)

    def read(self, r):
        if self.two_hops:
            return self.mid[r % RESIDUE_RADIX, self._mid_rows(r), :]
        return self.stage[pl.ds(r, Q_BLOCK, stride=self.dil), :]

    def write(self, r, value):
        if self.two_hops:
            self.mid[r % RESIDUE_RADIX, self._mid_rows(r), :] = value
        else:
            self.stage[pl.ds(r, Q_BLOCK, stride=self.dil), :] = value

    def store(self):
        if self.two_hops:
            for low in range(RESIDUE_RADIX):
                self.stage[self._hop_rows(low), :] = self.mid[low]
        return self.stage[...]


def _dil_attn_strided_kernel(q_ref, k_ref, v_ref, o_ref, l_ref, kd_ref, vd_ref, *stage_refs,
                             dil, group):
    first_span = pl.program_id(1) == 0

    @pl.when(first_span)
    def _():
        kd_ref[...] = jnp.zeros_like(kd_ref)
        vd_ref[...] = jnp.zeros_like(vd_ref)

    refs = list(stage_refs)
    qs, ks, vs, os_, ls = (
        _ResidueStage(refs.pop(0), refs.pop(0) if dil > RESIDUE_RADIX else None, dil)
        for _ in range(5))
    _, valid_first, dist = _band_masks(first_span, dil)
    for hh in range(HEADS_PER_GROUP):
        hs = slice(hh * HEAD_DIM, (hh + 1) * HEAD_DIM)
        bias = -_alibi_slope(group, hh) * dist
        qs.load(q_ref[:, hs].astype(F32))
        ks.load(k_ref[:, hs].astype(F32))
        vs.load(v_ref[:, hs].astype(F32))

        for r in range(dil):
            k_cur = ks.read(r).astype(BF16)
            v_cur = vs.read(r).astype(BF16)
            k = jnp.concatenate([kd_ref[hh, r], k_cur], axis=0)
            v = jnp.concatenate([vd_ref[hh, r], v_cur], axis=0)
            out, lse = _attend(qs.read(r).astype(BF16), k, v, bias, valid_first)
            kd_ref[hh, r] = k_cur
            vd_ref[hh, r] = v_cur
            os_.write(r, out)
            ls.write(r, jnp.broadcast_to(lse, (Q_BLOCK, HEAD_DIM)))
        o_ref[:, hs] = os_.store().astype(o_ref.dtype)
        l_ref[:, hs] = ls.store()


def _dil_attn(proj, kv, group, *, dense_rows=1024):
    _, dil = DIL_GROUPS[group]
    bsz, s, _ = proj.shape
    v_off = kv.shape[2] // (2 * GROUP_WIDTH)
    assert s % (dil * Q_BLOCK) == 0
    if dil == 1:
        rows = _row_tile(s, dense_rows)
        ratio = rows // Q_BLOCK
        prev = lambda col: pl.BlockSpec((None, Q_BLOCK, GROUP_WIDTH),
                                        lambda b, n: (b, jnp.maximum(n * ratio - 1, 0), col))
        body = functools.partial(_dil_attn_dense_kernel, group=group)
        operands, scratch = ("q", "kp", "k", "vp", "v"), []
    else:
        rows = dil * Q_BLOCK
        body = functools.partial(_dil_attn_strided_kernel, dil=dil, group=group)
        operands = ("q", "k", "v")
        split = pltpu.VMEM((HEADS_PER_GROUP, dil, Q_BLOCK, HEAD_DIM), BF16)
        stage = [pltpu.VMEM((rows, HEAD_DIM), F32)]
        if dil > RESIDUE_RADIX:
            stage.append(pltpu.VMEM((RESIDUE_RADIX, rows // RESIDUE_RADIX, HEAD_DIM), F32))
        scratch = [split, split] + stage * 5
    cur = lambda col: pl.BlockSpec((None, rows, GROUP_WIDTH), lambda b, n: (b, n, col))
    specs = {"q": cur(group), "k": cur(group), "v": cur(v_off + group)}
    if dil == 1:
        specs.update(kp=prev(group), vp=prev(v_off + group))
    return pl.pallas_call(
        body,
        grid=(bsz, s // rows),
        in_specs=[specs[name] for name in operands],
        out_specs=[cur(0), cur(0)],
        out_shape=[jax.ShapeDtypeStruct((bsz, s, GROUP_WIDTH), BF16),
                   jax.ShapeDtypeStruct((bsz, s, GROUP_WIDTH), F32)],
        scratch_shapes=scratch,
        compiler_params=_params("arbitrary", "arbitrary"),
        name=f"dil_attn_g{group}",
    )(*[proj if name == "q" else kv for name in operands])


def _mem_attn(q, kv_ref):
    outs = []
    for hh in range(MEM_HEADS):
        hs = slice(hh * HEAD_DIM, (hh + 1) * HEAD_DIM)
        vs = slice(MEM_WIDTH + hh * HEAD_DIM, MEM_WIDTH + (hh + 1) * HEAD_DIM)
        sc = lax.dot_general(q[:, hs], kv_ref[:, hs], (((1,), (1,)), ((), ())),
                             preferred_element_type=F32) * (HEAD_DIM ** -0.5)
        e = jnp.exp(sc - jnp.max(sc, axis=-1, keepdims=True))
        den = jnp.sum(e, axis=-1, keepdims=True)
        outs.append(jnp.dot(e.astype(BF16), kv_ref[:, vs], preferred_element_type=F32) / den)
    return jnp.concatenate(outs, axis=-1)


def _mix_out_tail(mixer, q_ref, kv_ref, w_ref, g_ref, h_ref, o_ref):
    m = _mem_attn(q_ref[...], kv_ref)
    mix = jnp.concatenate([mixer, m.astype(BF16)], axis=-1)
    out = jnp.dot(mix, w_ref[...], preferred_element_type=F32)
    o_ref[...] = h_ref[...] + _rms(out, g_ref[...])


def _mix_out_rec_kernel(y_ref, q_ref, kv_ref, w_ref, g_ref, h_ref, o_ref):
    _mix_out_tail(y_ref[...], q_ref, kv_ref, w_ref, g_ref, h_ref, o_ref)


def _mix_out_dil_kernel(o0_ref, o1_ref, o2_ref, l0_ref, l1_ref, l2_ref,
                        q_ref, kv_ref, w_ref, g_ref, h_ref, o_ref):
    lses = [l0_ref[...], l1_ref[...], l2_ref[...]]
    mx = jnp.maximum(jnp.maximum(lses[0], lses[1]), lses[2])
    es = [jnp.exp(l - mx) for l in lses]
    den = es[0] + es[1] + es[2]
    parts = [(o_ref_g[...].astype(F32) * (e / den)).astype(BF16)
             for o_ref_g, e in zip((o0_ref, o1_ref, o2_ref), es)]
    _mix_out_tail(jnp.concatenate(parts, axis=-1), q_ref, kv_ref, w_ref, g_ref, h_ref, o_ref)


def _mix_out(mixer_inputs, proj, memkv, w_out, post_g, h, layer, *, tm=512):
    bsz, s, d = h.shape
    tm = _row_tile(s, tm)
    q_block = proj.shape[2] // MEM_WIDTH - 1
    row_spec = lambda width, col=0: pl.BlockSpec((None, tm, width), lambda b, i: (b, i, col))
    kernel = _mix_out_rec_kernel if len(mixer_inputs) == 1 else _mix_out_dil_kernel
    return pl.pallas_call(
        kernel,
        grid=(bsz, s // tm),
        in_specs=[row_spec(a.shape[2]) for a in mixer_inputs] + [
            row_spec(MEM_WIDTH, q_block),
            pl.BlockSpec((None,) + memkv.shape[1:], lambda b, i: (b, 0, 0)),
            _resident((None,) + w_out.shape[1:], lambda b, i: (layer, 0, 0)),
            _resident((1, d), lambda b, i: (0, 0)),
            row_spec(d),
        ],
        out_specs=row_spec(d),
        out_shape=jax.ShapeDtypeStruct((bsz, s, d), F32),
        compiler_params=_params("arbitrary", "arbitrary"),
        name="mix_out_rec" if len(mixer_inputs) == 1 else "mix_out_dil",
    )(*mixer_inputs, proj, memkv, w_out, post_g.reshape(1, d), h)


def kernel(x, mem, mem_norm_g, a_pre_mix_g, a_post_mix_g, a_pre_ffn_g, a_post_ffn_g, a_w_in, a_conv_w, a_conv_b, a_gate_a_w, a_gate_a_b, a_gate_x_w, a_gate_x_b, a_lambda, a_w_mem_kv, a_w_out, a_w_ffn_in, a_w_ffn_out, kv_norm_g, w_kv_shared, b_pre_mix_g, b_post_mix_g, b_pre_ffn_g, b_post_ffn_g, b_w_in, b_w_mem_kv, b_w_out, b_w_ffn_in, b_w_ffn_out):
    bsz, s, d = x.shape
    n_mem = mem.shape[1]
    n_a = a_w_in.shape[0]
    n_b = b_w_in.shape[0]
    bf = lambda w: w.astype(BF16)
    a_w_in, a_gate_a_w, a_gate_x_w, a_w_mem_kv, b_w_mem_kv = map(
        bf, (a_w_in, a_gate_a_w, a_gate_x_w, a_w_mem_kv, b_w_mem_kv))

    def flat(t):
        return t.reshape(bsz * s, t.shape[-1])

    def mem_kv(w, l):
        kv_mem = _norm_matmul(mem.reshape(bsz * n_mem, d), mem_norm_g, w, l)
        return kv_mem.reshape(bsz, n_mem, -1)

    def ffn(h, pre_g, post_g, w_in, w_out):
        return _ffn(flat(h), pre_g, post_g, w_in, w_out, 0).reshape(bsz, s, d)

    h = x
    for l in range(n_a):
        proj = _norm_matmul(flat(h), a_pre_mix_g[l], a_w_in, l).reshape(bsz, s, -1)
        casts = [(a_w_ffn_in, l), (a_w_ffn_out, l), (a_w_out, l)]
        if l == n_a - 1:
            casts += [(w_kv_shared[None], 0), (b_w_in.reshape(1, n_b * d, -1), 0)]
        y, cast = _rglru(proj, a_conv_w[l], a_conv_b[l], a_gate_a_w, a_gate_a_b[l],
                         a_gate_x_w, a_gate_x_b[l], a_lambda[l], l, casts)
        h = _mix_out([y], proj, mem_kv(a_w_mem_kv, l), cast[2], a_post_mix_g[l], h, 0)
        h = ffn(h, a_pre_ffn_g[l], a_post_ffn_g[l], cast[0], cast[1])
    w_kv_bf, b_w_in_bf = cast[3], cast[4].reshape(b_w_in.shape)

    kv = _norm_matmul(flat(h), kv_norm_g, w_kv_bf, 0).reshape(bsz, s, -1)
    for l in range(n_b):
        proj, cast = _norm_matmul(flat(h), b_pre_mix_g[l], b_w_in_bf, l,
                                  [(b_w_ffn_in, l), (b_w_ffn_out, l), (b_w_out, l)])
        proj = proj.reshape(bsz, s, -1)
        outs, lses = zip(*[_dil_attn(proj, kv, g) for g in range(len(DIL_GROUPS))])
        h = _mix_out(list(outs) + list(lses), proj, mem_kv(b_w_mem_kv, l), cast[2],
                     b_post_mix_g[l], h, 0)
        h = ffn(h, b_pre_ffn_g[l], b_post_ffn_g[l], cast[0], cast[1])
    return h
```

```python
import functools
import math

import jax
import jax.numpy as jnp
from jax import lax
from jax.experimental import pallas as pl
from jax.experimental.pallas import tpu as pltpu

HEAD_DIM = 128
MEM_HEADS = 4
MEM_WIDTH = MEM_HEADS * HEAD_DIM
CONV_WIDTH = 4
LRU_C = 8.0
DIL_GROUPS = ((128, 1), (512, 4), (2048, 16))
HEADS_PER_GROUP = 4
GROUP_WIDTH = HEADS_PER_GROUP * HEAD_DIM
Q_BLOCK = 128
RMS_EPS = 1e-6
NEG_INF = -1e30

SUBLANES = 8
NORM_CHUNK_ROWS = 2 * SUBLANES
V7X_VMEM_BYTES = 64 * 1024 * 1024
VMEM_LIMIT_BYTES = V7X_VMEM_BYTES - 8 * 1024 * 1024
FFN_VMEM_LIMIT_BYTES = V7X_VMEM_BYTES - 2 * 1024 * 1024

F32 = jnp.float32
BF16 = jnp.bfloat16


def _params(*semantics, vmem_limit_bytes=VMEM_LIMIT_BYTES):
    return pltpu.CompilerParams(dimension_semantics=semantics, vmem_limit_bytes=vmem_limit_bytes)


def _rms(x, g):
    return x * lax.rsqrt(jnp.mean(x * x, axis=-1, keepdims=True) + RMS_EPS) * g


def _row_tile(rows, target):
    t = min(rows, target)
    assert rows % t == 0, (rows, t)
    return t


BF16_ROW_TILE = 16


def _side_cast_specs(casts, n_steps, step_of):
    in_specs, out_specs, out_shapes = [], [], []
    for w, layer in casts:
        _, rows, cols = w.shape
        slab = rows // n_steps
        assert rows % n_steps == 0 and slab % BF16_ROW_TILE == 0, (w.shape, n_steps)
        in_specs.append(pl.BlockSpec(
            (None, slab, cols), lambda *ids, layer=layer: (layer, step_of(*ids), 0)))
        out_specs.append(pl.BlockSpec((None, slab, cols), lambda *ids: (0, step_of(*ids), 0)))
        out_shapes.append(jax.ShapeDtypeStruct((1, rows, cols), BF16))
    return in_specs, out_specs, out_shapes


def _side_cast(src_refs, dst_refs):
    for src, dst in zip(src_refs, dst_refs):
        dst[...] = src[...].astype(dst.dtype)


def _norm_matmul_kernel(h_ref, g_ref, w_ref, *refs, tn):
    n_cast = len(refs) // 2
    o_ref = refs[n_cast]
    _side_cast(refs[:n_cast], refs[n_cast + 1:])
    hn = _rms(h_ref[...], g_ref[...]).astype(BF16)
    for c in range(0, o_ref.shape[1], tn):
        o_ref[:, c:c + tn] = jnp.dot(hn, w_ref[:, c:c + tn],
                                     preferred_element_type=F32).astype(o_ref.dtype)


def _resident(block_shape, index_map):
    return pl.BlockSpec(block_shape, index_map, pipeline_mode=pl.Buffered(1))


def _norm_matmul(h, g, w, layer, casts=(), *, tm=512, tn=512):
    m, d = h.shape
    n = w.shape[2]
    tm = _row_tile(m, tm)
    tn = _row_tile(n, tn)
    cast_in, cast_out, cast_shapes = _side_cast_specs(casts, m // tm, lambda i: i)
    out = pl.pallas_call(
        functools.partial(_norm_matmul_kernel, tn=tn),
        grid=(m // tm,),
        in_specs=[
            pl.BlockSpec((tm, d), lambda i: (i, 0)),
            _resident((1, d), lambda i: (0, 0)),
            _resident((None, d, n), lambda i: (layer, 0, 0)),
        ] + cast_in,
        out_specs=[pl.BlockSpec((tm, n), lambda i: (i, 0))] + cast_out,
        out_shape=[jax.ShapeDtypeStruct((m, n), BF16)] + cast_shapes,
        compiler_params=_params("arbitrary"),
        name="norm_matmul",
    )(h, g.reshape(1, d), w, *[cw for cw, _ in casts])
    return (out[0], out[1:]) if casts else out[0]


def _ffn_kernel(h_ref, pre_g_ref, post_g_ref, wg_ref, wu_ref, wo_ref, o_ref, hn_ref):
    j = pl.program_id(1)

    @pl.when(j == 0)
    def _():
        hn_ref[...] = _rms(h_ref[...], pre_g_ref[...]).astype(hn_ref.dtype)
        o_ref[...] = jnp.zeros_like(o_ref)

    hn = hn_ref[...]
    gate = jnp.dot(hn, wg_ref[...], preferred_element_type=F32)
    up = jnp.dot(hn, wu_ref[...], preferred_element_type=F32)
    act = (gate * jax.nn.sigmoid(gate) * up).astype(BF16)
    o_ref[...] += jnp.dot(act, wo_ref[...], preferred_element_type=F32)

    @pl.when(j == pl.num_programs(1) - 1)
    def _():
        for r in range(0, o_ref.shape[0], NORM_CHUNK_ROWS):
            rows = slice(r, r + NORM_CHUNK_ROWS)
            o_ref[rows, :] = h_ref[rows, :] + _rms(o_ref[rows, :], post_g_ref[...])


def _ffn(h, pre_g, post_g, w_in, w_out, layer, *, tm=1024, tf=512):
    m, d = h.shape
    f = w_out.shape[1]
    tm = _row_tile(m, tm)
    tf = _row_tile(f, tf)
    nf = f // tf
    return pl.pallas_call(
        _ffn_kernel,
        grid=(m // tm, nf),
        in_specs=[
            pl.BlockSpec((tm, d), lambda i, j: (i, 0)),
            pl.BlockSpec((1, d), lambda i, j: (0, 0)),
            pl.BlockSpec((1, d), lambda i, j: (0, 0)),
            pl.BlockSpec((None, d, tf), lambda i, j: (layer, 0, j)),
            pl.BlockSpec((None, d, tf), lambda i, j: (layer, 0, j + nf)),
            pl.BlockSpec((None, tf, d), lambda i, j: (layer, j, 0)),
        ],
        out_specs=pl.BlockSpec((tm, d), lambda i, j: (i, 0)),
        out_shape=jax.ShapeDtypeStruct((m, d), F32),
        scratch_shapes=[pltpu.VMEM((tm, d), BF16)],
        compiler_params=_params("arbitrary", "arbitrary", vmem_limit_bytes=FFN_VMEM_LIMIT_BYTES),
        name="ffn",
    )(h, pre_g.reshape(1, d), post_g.reshape(1, d), w_in, w_in, w_out)


LRU_PAD = SUBLANES


def _lru_steps(first, count):
    return pl.ds(LRU_PAD + first, count, stride=1)


def _rglru_reset(xs_ref, hprev_ref, ts):
    hprev_ref[...] = jnp.zeros_like(hprev_ref)
    for n in range(xs_ref.shape[0]):
        xs_ref[n, _lru_steps(ts - LRU_PAD, LRU_PAD), :] = jnp.zeros((LRU_PAD, HEAD_DIM), F32)


def _rglru_tile(x_of, gate_of, y_ref, cw_ref, cb_ref, wa_ref, ba_ref, wx_ref, bx_ref, lam_ref,
                xs_ref, as_ref, bs_ref, hprev_ref, ts):
    pad, steps = LRU_PAD, _lru_steps
    n_blocks = cw_ref.shape[1] // HEAD_DIM

    for n in range(n_blocks):
        xs_ref[n, steps(-pad, pad), :] = xs_ref[n, steps(ts - pad, pad), :]
        as_ref[n, steps(-pad, pad), :] = jnp.ones((pad, HEAD_DIM), F32)
        bs_ref[n, steps(-pad, pad), :] = jnp.zeros((pad, HEAD_DIM), F32)

    for n in range(n_blocks):
        sl = slice(n * HEAD_DIM, (n + 1) * HEAD_DIM)
        x = x_of(n).astype(F32)
        xs_ref[n, steps(0, ts), :] = x
        cw = cw_ref[:, sl]
        xc = cb_ref[:, sl] + cw[CONV_WIDTH - 1:CONV_WIDTH, :] * x
        for s in range(1, CONV_WIDTH):
            xc = xc + cw[CONV_WIDTH - 1 - s:CONV_WIDTH - s, :] * xs_ref[n, steps(-s, ts), :]

        xcb = xc.astype(BF16)
        r2 = 1.0 + jnp.tanh(
            0.5 * (jnp.dot(xcb, wa_ref[n], preferred_element_type=F32) + ba_ref[:, sl]))
        i2 = 1.0 + jnp.tanh(
            0.5 * (jnp.dot(xcb, wx_ref[n], preferred_element_type=F32) + bx_ref[:, sl]))
        neg_lam = -lam_ref[:, sl]
        softplus = jnp.maximum(neg_lam, 0.0) + jnp.log1p(jnp.exp(-jnp.abs(neg_lam)))
        a = jnp.exp2(r2 * ((-0.5 * LRU_C * math.log2(math.e)) * softplus))
        z = 1.0 - a * a
        b = (0.5 * jnp.where(z > 0.0, z * lax.rsqrt(z), 0.0)) * (i2 * xc)

        for shift in (1, 2, 4):
            as_ref[n, steps(0, ts), :] = a
            bs_ref[n, steps(0, ts), :] = b
            b = a * bs_ref[n, steps(-shift, ts), :] + b
            a = a * as_ref[n, steps(-shift, ts), :]
        hg = a[:SUBLANES] * hprev_ref[:, sl] + b[:SUBLANES]
        groups = [hg]
        for v in range(1, ts // SUBLANES):
            rows = slice(v * SUBLANES, (v + 1) * SUBLANES)
            hg = a[rows] * hg + b[rows]
            groups.append(hg)
        h = jnp.concatenate(groups, axis=0)
        hprev_ref[:, sl] = hg[SUBLANES - 1:, :]

        gb = gate_of(n).astype(F32)
        k = math.sqrt(2.0 / math.pi)
        cdf = 0.5 * (1.0 + jnp.tanh(gb * (k + (k * 0.044715) * (gb * gb))))
        y_ref[:, sl] = (h * (gb * cdf)).astype(y_ref.dtype)


def _rglru_kernel(xb_ref, gb_ref, cw_ref, cb_ref, wa_ref, ba_ref, wx_ref, bx_ref, lam_ref,
                  *refs, n_cast):
    cast_in, y_ref, cast_out = refs[:n_cast], refs[n_cast], refs[n_cast + 1:2 * n_cast + 1]
    xs_ref, as_ref, bs_ref, hprev_ref = refs[2 * n_cast + 1:]
    ts = xb_ref.shape[0]
    _side_cast(cast_in, cast_out)

    @pl.when(pl.program_id(1) == 0)
    def _():
        _rglru_reset(xs_ref, hprev_ref, ts)

    _rglru_tile(lambda n: xb_ref[:, n * HEAD_DIM:(n + 1) * HEAD_DIM],
                lambda n: gb_ref[:, n * HEAD_DIM:(n + 1) * HEAD_DIM],
                y_ref, cw_ref, cb_ref, wa_ref, ba_ref, wx_ref, bx_ref, lam_ref,
                xs_ref, as_ref, bs_ref, hprev_ref, ts)


def _rglru(proj, conv_w, conv_b, wa, ba, wx, bx, lam, layer, casts=(), *, ts=512):
    bsz, s, _ = proj.shape
    c = conv_w.shape[1]
    ts = _row_tile(s, ts)
    nt = s // ts
    vec = lambda v: v.reshape(1, c)
    full2 = lambda shape: pl.BlockSpec(shape, lambda b, i: (0, 0))
    gate_w = lambda w: pl.BlockSpec((None,) + w.shape[1:], lambda b, i: (layer, 0, 0, 0))
    stage = pltpu.VMEM((c // HEAD_DIM, ts + LRU_PAD, HEAD_DIM), F32)
    cast_in, cast_out, cast_shapes = _side_cast_specs(casts, bsz * nt, lambda b, i: b * nt + i)
    out = pl.pallas_call(
        functools.partial(_rglru_kernel, n_cast=len(casts)),
        grid=(bsz, nt),
        in_specs=[
            pl.BlockSpec((None, ts, c), lambda b, i: (b, i, 0)),
            pl.BlockSpec((None, ts, c), lambda b, i: (b, i, 1)),
            full2((CONV_WIDTH, c)), full2((1, c)),
            gate_w(wa), full2((1, c)),
            gate_w(wx), full2((1, c)),
            full2((1, c)),
        ] + cast_in,
        out_specs=[pl.BlockSpec((None, ts, c), lambda b, i: (b, i, 0))] + cast_out,
        out_shape=[jax.ShapeDtypeStruct((bsz, s, c), BF16)] + cast_shapes,
        scratch_shapes=[stage, stage, stage, pltpu.VMEM((1, c), F32)],
        compiler_params=_params("arbitrary", "arbitrary"),
        name="rglru",
    )(proj, proj, conv_w, vec(conv_b), wa, vec(ba), wx, vec(bx), vec(lam),
      *[cw for cw, _ in casts])
    return out[0], out[1:]


def _band_masks(first_block, dil):
    qi = lax.broadcasted_iota(jnp.int32, (Q_BLOCK, 2 * Q_BLOCK), 0)
    ki = lax.broadcasted_iota(jnp.int32, (Q_BLOCK, 2 * Q_BLOCK), 1)
    rel = qi + Q_BLOCK - ki
    valid = (rel >= 0) & (rel <= Q_BLOCK)
    valid_first = valid & (ki >= jnp.where(first_block, Q_BLOCK, 0))
    return valid, valid_first, (rel * dil).astype(F32)


def _alibi_slope(group, hh):
    n_heads = len(DIL_GROUPS) * HEADS_PER_GROUP
    return 2.0 ** (-8.0 * (group * HEADS_PER_GROUP + hh + 1) / n_heads)


def _attend(q, k, v, bias, mask):
    sc = lax.dot_general(q, k, (((1,), (1,)), ((), ())),
                         preferred_element_type=F32) * (HEAD_DIM ** -0.5)
    sc = jnp.where(mask, sc + bias, NEG_INF)
    mx = jnp.max(sc, axis=-1, keepdims=True)
    e = jnp.exp(sc - mx)
    den = jnp.sum(e, axis=-1, keepdims=True)
    out = jnp.dot(e.astype(BF16), v, preferred_element_type=F32) / den
    return out, mx + jnp.log(den)


def _dil_attn_dense_kernel(q_ref, kp_ref, kc_ref, vp_ref, vc_ref, o_ref, l_ref, *, group):
    valid, valid_first, dist = _band_masks(pl.program_id(1) == 0, 1)
    for hh in range(HEADS_PER_GROUP):
        hs = slice(hh * HEAD_DIM, (hh + 1) * HEAD_DIM)
        bias = -_alibi_slope(group, hh) * dist
        for j in range(q_ref.shape[0] // Q_BLOCK):
            cur = slice(j * Q_BLOCK, (j + 1) * Q_BLOCK)
            if j == 0:
                k_prev, v_prev, mask = kp_ref[:, hs], vp_ref[:, hs], valid_first
            else:
                prev = slice((j - 1) * Q_BLOCK, j * Q_BLOCK)
                k_prev, v_prev, mask = kc_ref[prev, hs], vc_ref[prev, hs], valid
            k = jnp.concatenate([k_prev, kc_ref[cur, hs]], axis=0)
            v = jnp.concatenate([v_prev, vc_ref[cur, hs]], axis=0)
            out, lse = _attend(q_ref[cur, hs], k, v, bias, mask)
            o_ref[cur, hs] = out.astype(o_ref.dtype)
            l_ref[cur, hs] = jnp.broadcast_to(lse, (Q_BLOCK, HEAD_DIM))


RESIDUE_RADIX = 4


class _ResidueStage:
    def __init__(self, stage_ref, mid_ref, dil):
        self.stage, self.mid, self.dil = stage_ref, mid_ref, dil
        self.two_hops = dil > RESIDUE_RADIX
        assert dil == RESIDUE_RADIX or dil == RESIDUE_RADIX ** 2

    def _hop_rows(self, low):
        return pl.ds(low, self.stage.shape[0] // RESIDUE_RADIX, stride=RESIDUE_RADIX)

    def load(self, value):
        self.stage[...] = value
        if self.two_hops:
            for low in range(RESIDUE_RADIX):
                self.mid[low] = self.stage[self._hop_rows(low), :]

    def _mid_rows(self, r):
        return pl.ds(r // RESIDUE_RADIX, Q_BLOCK, stride=self.dil // RESIDUE_RADIX)

    def read(self, r):
        if self.two_hops:
            return self.mid[r % RESIDUE_RADIX, self._mid_rows(r), :]
        return self.stage[pl.ds(r, Q_BLOCK, stride=self.dil), :]

    def write(self, r, value):
        if self.two_hops:
            self.mid[r % RESIDUE_RADIX, self._mid_rows(r), :] = value
        else:
            self.stage[pl.ds(r, Q_BLOCK, stride=self.dil), :] = value

    def store(self):
        if self.two_hops:
            for low in range(RESIDUE_RADIX):
                self.stage[self._hop_rows(low), :] = self.mid[low]
        return self.stage[...]


def _dil_attn_strided_kernel(q_ref, k_ref, v_ref, o_ref, l_ref, kd_ref, vd_ref, *stage_refs,
                             dil, group):
    first_span = pl.program_id(1) == 0

    @pl.when(first_span)
    def _():
        kd_ref[...] = jnp.zeros_like(kd_ref)
        vd_ref[...] = jnp.zeros_like(vd_ref)

    refs = list(stage_refs)
    qs, ks, vs, os_, ls = (
        _ResidueStage(refs.pop(0), refs.pop(0) if dil > RESIDUE_RADIX else None, dil)
        for _ in range(5))
    _, valid_first, dist = _band_masks(first_span, dil)
    for hh in range(HEADS_PER_GROUP):
        hs = slice(hh * HEAD_DIM, (hh + 1) * HEAD_DIM)
        bias = -_alibi_slope(group, hh) * dist
        qs.load(q_ref[:, hs].astype(F32))
        ks.load(k_ref[:, hs].astype(F32))
        vs.load(v_ref[:, hs].astype(F32))

        for r in range(dil):
            k_cur = ks.read(r).astype(BF16)
            v_cur = vs.read(r).astype(BF16)
            k = jnp.concatenate([kd_ref[hh, r], k_cur], axis=0)
            v = jnp.concatenate([vd_ref[hh, r], v_cur], axis=0)
            out, lse = _attend(qs.read(r).astype(BF16), k, v, bias, valid_first)
            kd_ref[hh, r] = k_cur
            vd_ref[hh, r] = v_cur
            os_.write(r, out)
            ls.write(r, jnp.broadcast_to(lse, (Q_BLOCK, HEAD_DIM)))
        o_ref[:, hs] = os_.store().astype(o_ref.dtype)
        l_ref[:, hs] = ls.store()


def _dil_attn(proj, kv, group, *, dense_rows=1024):
    _, dil = DIL_GROUPS[group]
    bsz, s, _ = proj.shape
    v_off = kv.shape[2] // (2 * GROUP_WIDTH)
    assert s % (dil * Q_BLOCK) == 0
    if dil == 1:
        rows = _row_tile(s, dense_rows)
        ratio = rows // Q_BLOCK
        prev = lambda col: pl.BlockSpec((None, Q_BLOCK, GROUP_WIDTH),
                                        lambda b, n: (b, jnp.maximum(n * ratio - 1, 0), col))
        body = functools.partial(_dil_attn_dense_kernel, group=group)
        operands, scratch = ("q", "kp", "k", "vp", "v"), []
    else:
        rows = dil * Q_BLOCK
        body = functools.partial(_dil_attn_strided_kernel, dil=dil, group=group)
        operands = ("q", "k", "v")
        split = pltpu.VMEM((HEADS_PER_GROUP, dil, Q_BLOCK, HEAD_DIM), BF16)
        stage = [pltpu.VMEM((rows, HEAD_DIM), F32)]
        if dil > RESIDUE_RADIX:
            stage.append(pltpu.VMEM((RESIDUE_RADIX, rows // RESIDUE_RADIX, HEAD_DIM), F32))
        scratch = [split, split] + stage * 5
    cur = lambda col: pl.BlockSpec((None, rows, GROUP_WIDTH), lambda b, n: (b, n, col))
    specs = {"q": cur(group), "k": cur(group), "v": cur(v_off + group)}
    if dil == 1:
        specs.update(kp=prev(group), vp=prev(v_off + group))
    return pl.pallas_call(
        body,
        grid=(bsz, s // rows),
        in_specs=[specs[name] for name in operands],
        out_specs=[cur(0), cur(0)],
        out_shape=[jax.ShapeDtypeStruct((bsz, s, GROUP_WIDTH), BF16),
                   jax.ShapeDtypeStruct((bsz, s, GROUP_WIDTH), F32)],
        scratch_shapes=scratch,
        compiler_params=_params("arbitrary", "arbitrary"),
        name=f"dil_attn_g{group}",
    )(*[proj if name == "q" else kv for name in operands])


def _mem_attn(q, kv_ref):
    outs = []
    for hh in range(MEM_HEADS):
        hs = slice(hh * HEAD_DIM, (hh + 1) * HEAD_DIM)
        vs = slice(MEM_WIDTH + hh * HEAD_DIM, MEM_WIDTH + (hh + 1) * HEAD_DIM)
        sc = lax.dot_general(q[:, hs], kv_ref[:, hs], (((1,), (1,)), ((), ())),
                             preferred_element_type=F32) * (HEAD_DIM ** -0.5)
        e = jnp.exp(sc - jnp.max(sc, axis=-1, keepdims=True))
        den = jnp.sum(e, axis=-1, keepdims=True)
        outs.append(jnp.dot(e.astype(BF16), kv_ref[:, vs], preferred_element_type=F32) / den)
    return jnp.concatenate(outs, axis=-1)


def _mix_out_tail(mixer, q_ref, kv_ref, w_ref, g_ref, h_ref, o_ref):
    m = _mem_attn(q_ref[...], kv_ref)
    mix = jnp.concatenate([mixer, m.astype(BF16)], axis=-1)
    out = jnp.dot(mix, w_ref[...], preferred_element_type=F32)
    o_ref[...] = h_ref[...] + _rms(out, g_ref[...])


def _mix_out_rec_kernel(y_ref, q_ref, kv_ref, w_ref, g_ref, h_ref, o_ref):
    _mix_out_tail(y_ref[...], q_ref, kv_ref, w_ref, g_ref, h_ref, o_ref)


def _mix_out_dil_kernel(o0_ref, o1_ref, o2_ref, l0_ref, l1_ref, l2_ref,
                        q_ref, kv_ref, w_ref, g_ref, h_ref, o_ref):
    lses = [l0_ref[...], l1_ref[...], l2_ref[...]]
    mx = jnp.maximum(jnp.maximum(lses[0], lses[1]), lses[2])
    es = [jnp.exp(l - mx) for l in lses]
    den = es[0] + es[1] + es[2]
    parts = [(o_ref_g[...].astype(F32) * (e / den)).astype(BF16)
             for o_ref_g, e in zip((o0_ref, o1_ref, o2_ref), es)]
    _mix_out_tail(jnp.concatenate(parts, axis=-1), q_ref, kv_ref, w_ref, g_ref, h_ref, o_ref)


def _mix_out(mixer_inputs, proj, memkv, w_out, post_g, h, layer, *, tm=512):
    bsz, s, d = h.shape
    tm = _row_tile(s, tm)
    q_block = proj.shape[2] // MEM_WIDTH - 1
    row_spec = lambda width, col=0: pl.BlockSpec((None, tm, width), lambda b, i: (b, i, col))
    kernel = _mix_out_rec_kernel if len(mixer_inputs) == 1 else _mix_out_dil_kernel
    return pl.pallas_call(
        kernel,
        grid=(bsz, s // tm),
        in_specs=[row_spec(a.shape[2]) for a in mixer_inputs] + [
            row_spec(MEM_WIDTH, q_block),
            pl.BlockSpec((None,) + memkv.shape[1:], lambda b, i: (b, 0, 0)),
            _resident((None,) + w_out.shape[1:], lambda b, i: (layer, 0, 0)),
            _resident((1, d), lambda b, i: (0, 0)),
            row_spec(d),
        ],
        out_specs=row_spec(d),
        out_shape=jax.ShapeDtypeStruct((bsz, s, d), F32),
        compiler_params=_params("arbitrary", "arbitrary"),
        name="mix_out_rec" if len(mixer_inputs) == 1 else "mix_out_dil",
    )(*mixer_inputs, proj, memkv, w_out, post_g.reshape(1, d), h)


def kernel(x, mem, mem_norm_g, a_pre_mix_g, a_post_mix_g, a_pre_ffn_g, a_post_ffn_g, a_w_in, a_conv_w, a_conv_b, a_gate_a_w, a_gate_a_b, a_gate_x_w, a_gate_x_b, a_lambda, a_w_mem_kv, a_w_out, a_w_ffn_in, a_w_ffn_out, kv_norm_g, w_kv_shared, b_pre_mix_g, b_post_mix_g, b_pre_ffn_g, b_post_ffn_g, b_w_in, b_w_mem_kv, b_w_out, b_w_ffn_in, b_w_ffn_out):
    bsz, s, d = x.shape
    n_mem = mem.shape[1]
    n_a = a_w_in.shape[0]
    n_b = b_w_in.shape[0]
    bf = lambda w: w.astype(BF16)
    a_w_in, a_gate_a_w, a_gate_x_w, a_w_mem_kv, b_w_mem_kv = map(
        bf, (a_w_in, a_gate_a_w, a_gate_x_w, a_w_mem_kv, b_w_mem_kv))

    def flat(t):
        return t.reshape(bsz * s, t.shape[-1])

    def mem_kv(w, l):
        kv_mem = _norm_matmul(mem.reshape(bsz * n_mem, d), mem_norm_g, w, l)
        return kv_mem.reshape(bsz, n_mem, -1)

    def ffn(h, pre_g, post_g, w_in, w_out):
        return _ffn(flat(h), pre_g, post_g, w_in, w_out, 0).reshape(bsz, s, d)

    h = x
    for l in range(n_a):
        proj = _norm_matmul(flat(h), a_pre_mix_g[l], a_w_in, l).reshape(bsz, s, -1)
        casts = [(a_w_ffn_in, l), (a_w_ffn_out, l), (a_w_out, l)]
        if l == n_a - 1:
            casts += [(w_kv_shared[None], 0), (b_w_in.reshape(1, n_b * d, -1), 0)]
        y, cast = _rglru(proj, a_conv_w[l], a_conv_b[l], a_gate_a_w, a_gate_a_b[l],
                         a_gate_x_w, a_gate_x_b[l], a_lambda[l], l, casts)
        h = _mix_out([y], proj, mem_kv(a_w_mem_kv, l), cast[2], a_post_mix_g[l], h, 0)
        h = ffn(h, a_pre_ffn_g[l], a_post_ffn_g[l], cast[0], cast[1])
    w_kv_bf, b_w_in_bf = cast[3], cast[4].reshape(b_w_in.shape)

    kv = _norm_matmul(flat(h), kv_norm_g, w_kv_bf, 0).reshape(bsz, s, -1)
    for l in range(n_b):
        proj, cast = _norm_matmul(flat(h), b_pre_mix_g[l], b_w_in_bf, l,
                                  [(b_w_ffn_in, l), (b_w_ffn_out, l), (b_w_out, l)])
        proj = proj.reshape(bsz, s, -1)
        outs, lses = zip(*[_dil_attn(proj, kv, g) for g in range(len(DIL_GROUPS))])
        h = _mix_out(list(outs) + list(lses), proj, mem_kv(b_w_mem_kv, l), cast[2],
                     b_post_mix_g[l], h, 0)
        h = ffn(h, b_pre_ffn_g[l], b_post_ffn_g[l], cast[0], cast[1])
    return h
```

```python
import functools
import math

import jax
import jax.numpy as jnp
from jax import lax
from jax.experimental import pallas as pl
from jax.experimental.pallas import tpu as pltpu

HEAD_DIM = 128
MEM_HEADS = 4
MEM_WIDTH = MEM_HEADS * HEAD_DIM
CONV_WIDTH = 4
LRU_C = 8.0
DIL_GROUPS = ((128, 1), (512, 4), (2048, 16))
HEADS_PER_GROUP = 4
GROUP_WIDTH = HEADS_PER_GROUP * HEAD_DIM
Q_BLOCK = 128
RMS_EPS = 1e-6
NEG_INF = -1e30

SUBLANES = 8
NORM_CHUNK_ROWS = 2 * SUBLANES
V7X_VMEM_BYTES = 64 * 1024 * 1024
VMEM_LIMIT_BYTES = V7X_VMEM_BYTES - 8 * 1024 * 1024
FFN_VMEM_LIMIT_BYTES = V7X_VMEM_BYTES - 2 * 1024 * 1024

F32 = jnp.float32
BF16 = jnp.bfloat16


def _params(*semantics, vmem_limit_bytes=VMEM_LIMIT_BYTES):
    return pltpu.CompilerParams(dimension_semantics=semantics, vmem_limit_bytes=vmem_limit_bytes)


def _rms(x, g):
    return x * lax.rsqrt(jnp.mean(x * x, axis=-1, keepdims=True) + RMS_EPS) * g


def _row_tile(rows, target):
    t = min(rows, target)
    assert rows % t == 0, (rows, t)
    return t


BF16_ROW_TILE = 16


def _side_cast_specs(casts, n_steps, step_of):
    in_specs, out_specs, out_shapes = [], [], []
    for w, layer in casts:
        _, rows, cols = w.shape
        slab = rows // n_steps
        assert rows % n_steps == 0 and slab % BF16_ROW_TILE == 0, (w.shape, n_steps)
        in_specs.append(pl.BlockSpec(
            (None, slab, cols), lambda *ids, layer=layer: (layer, step_of(*ids), 0)))
        out_specs.append(pl.BlockSpec((None, slab, cols), lambda *ids: (0, step_of(*ids), 0)))
        out_shapes.append(jax.ShapeDtypeStruct((1, rows, cols), BF16))
    return in_specs, out_specs, out_shapes


def _side_cast(src_refs, dst_refs):
    for src, dst in zip(src_refs, dst_refs):
        dst[...] = src[...].astype(dst.dtype)


def _norm_matmul_kernel(h_ref, g_ref, w_ref, *refs, tn):
    n_cast = len(refs) // 2
    o_ref = refs[n_cast]
    _side_cast(refs[:n_cast], refs[n_cast + 1:])
    hn = _rms(h_ref[...], g_ref[...]).astype(BF16)
    for c in range(0, o_ref.shape[1], tn):
        o_ref[:, c:c + tn] = jnp.dot(hn, w_ref[:, c:c + tn],
                                     preferred_element_type=F32).astype(o_ref.dtype)


def _resident(block_shape, index_map):
    return pl.BlockSpec(block_shape, index_map, pipeline_mode=pl.Buffered(1))


def _norm_matmul(h, g, w, layer, casts=(), *, tm=512, tn=512):
    m, d = h.shape
    n = w.shape[2]
    tm = _row_tile(m, tm)
    tn = _row_tile(n, tn)
    cast_in, cast_out, cast_shapes = _side_cast_specs(casts, m // tm, lambda i: i)
    out = pl.pallas_call(
        functools.partial(_norm_matmul_kernel, tn=tn),
        grid=(m // tm,),
        in_specs=[
            pl.BlockSpec((tm, d), lambda i: (i, 0)),
            _resident((1, d), lambda i: (0, 0)),
            _resident((None, d, n), lambda i: (layer, 0, 0)),
        ] + cast_in,
        out_specs=[pl.BlockSpec((tm, n), lambda i: (i, 0))] + cast_out,
        out_shape=[jax.ShapeDtypeStruct((m, n), BF16)] + cast_shapes,
        compiler_params=_params("arbitrary"),
        name="norm_matmul",
    )(h, g.reshape(1, d), w, *[cw for cw, _ in casts])
    return (out[0], out[1:]) if casts else out[0]


def _ffn_kernel(h_ref, pre_g_ref, post_g_ref, wg_ref, wu_ref, wo_ref, o_ref, hn_ref):
    j = pl.program_id(1)

    @pl.when(j == 0)
    def _():
        hn_ref[...] = _rms(h_ref[...], pre_g_ref[...]).astype(hn_ref.dtype)
        o_ref[...] = jnp.zeros_like(o_ref)

    hn = hn_ref[...]
    half = wg_ref.shape[1] // 2
    acts = []
    for c in (0, half):
        gate = jnp.dot(hn, wg_ref[:, c:c + half], preferred_element_type=F32)
        up = jnp.dot(hn, wu_ref[:, c:c + half], preferred_element_type=F32)
        acts.append((gate * jax.nn.sigmoid(gate) * up).astype(BF16))
    o_ref[...] += jnp.dot(jnp.concatenate(acts, axis=-1), wo_ref[...],
                          preferred_element_type=F32)

    @pl.when(j == pl.num_programs(1) - 1)
    def _():
        for r in range(0, o_ref.shape[0], NORM_CHUNK_ROWS):
            rows = slice(r, r + NORM_CHUNK_ROWS)
            o_ref[rows, :] = h_ref[rows, :] + _rms(o_ref[rows, :], post_g_ref[...])


def _ffn(h, pre_g, post_g, w_in, w_out, layer, *, tm=1024, tf=512):
    m, d = h.shape
    f = w_out.shape[1]
    tm = _row_tile(m, tm)
    tf = _row_tile(f, tf)
    nf = f // tf
    return pl.pallas_call(
        _ffn_kernel,
        grid=(m // tm, nf),
        in_specs=[
            pl.BlockSpec((tm, d), lambda i, j: (i, 0)),
            pl.BlockSpec((1, d), lambda i, j: (0, 0)),
            pl.BlockSpec((1, d), lambda i, j: (0, 0)),
            pl.BlockSpec((None, d, tf), lambda i, j: (layer, 0, j)),
            pl.BlockSpec((None, d, tf), lambda i, j: (layer, 0, j + nf)),
            pl.BlockSpec((None, tf, d), lambda i, j: (layer, j, 0)),
        ],
        out_specs=pl.BlockSpec((tm, d), lambda i, j: (i, 0)),
        out_shape=jax.ShapeDtypeStruct((m, d), F32),
        scratch_shapes=[pltpu.VMEM((tm, d), BF16)],
        compiler_params=_params("arbitrary", "arbitrary", vmem_limit_bytes=FFN_VMEM_LIMIT_BYTES),
        name="ffn",
    )(h, pre_g.reshape(1, d), post_g.reshape(1, d), w_in, w_in, w_out)


LRU_PAD = SUBLANES


def _lru_steps(first, count):
    return pl.ds(LRU_PAD + first, count, stride=1)


def _rglru_reset(xs_ref, hprev_ref, ts):
    hprev_ref[...] = jnp.zeros_like(hprev_ref)
    for n in range(xs_ref.shape[0]):
        xs_ref[n, _lru_steps(ts - LRU_PAD, LRU_PAD), :] = jnp.zeros((LRU_PAD, HEAD_DIM), F32)


def _rglru_tile(x_of, gate_of, y_ref, cw_ref, cb_ref, wa_ref, ba_ref, wx_ref, bx_ref, lam_ref,
                xs_ref, as_ref, bs_ref, hprev_ref, ts):
    pad, steps = LRU_PAD, _lru_steps
    n_blocks = cw_ref.shape[1] // HEAD_DIM

    for n in range(n_blocks):
        xs_ref[n, steps(-pad, pad), :] = xs_ref[n, steps(ts - pad, pad), :]
        as_ref[n, steps(-pad, pad), :] = jnp.ones((pad, HEAD_DIM), F32)
        bs_ref[n, steps(-pad, pad), :] = jnp.zeros((pad, HEAD_DIM), F32)

    for n in range(n_blocks):
        sl = slice(n * HEAD_DIM, (n + 1) * HEAD_DIM)
        x = x_of(n).astype(F32)
        xs_ref[n, steps(0, ts), :] = x
        cw = cw_ref[:, sl]
        xc = cb_ref[:, sl] + cw[CONV_WIDTH - 1:CONV_WIDTH, :] * x
        for s in range(1, CONV_WIDTH):
            xc = xc + cw[CONV_WIDTH - 1 - s:CONV_WIDTH - s, :] * xs_ref[n, steps(-s, ts), :]

        xcb = xc.astype(BF16)
        r2 = 1.0 + jnp.tanh(
            0.5 * (jnp.dot(xcb, wa_ref[n], preferred_element_type=F32) + ba_ref[:, sl]))
        i2 = 1.0 + jnp.tanh(
            0.5 * (jnp.dot(xcb, wx_ref[n], preferred_element_type=F32) + bx_ref[:, sl]))
        neg_lam = -lam_ref[:, sl]
        softplus = jnp.maximum(neg_lam, 0.0) + jnp.log1p(jnp.exp(-jnp.abs(neg_lam)))
        a = jnp.exp2(r2 * ((-0.5 * LRU_C * math.log2(math.e)) * softplus))
        z = 1.0 - a * a
        b = (0.5 * jnp.where(z > 0.0, z * lax.rsqrt(z), 0.0)) * (i2 * xc)

        for shift in (1, 2, 4):
            as_ref[n, steps(0, ts), :] = a
            bs_ref[n, steps(0, ts), :] = b
            b = a * bs_ref[n, steps(-shift, ts), :] + b
            a = a * as_ref[n, steps(-shift, ts), :]
        hg = a[:SUBLANES] * hprev_ref[:, sl] + b[:SUBLANES]
        groups = [hg]
        for v in range(1, ts // SUBLANES):
            rows = slice(v * SUBLANES, (v + 1) * SUBLANES)
            hg = a[rows] * hg + b[rows]
            groups.append(hg)
        h = jnp.concatenate(groups, axis=0)
        hprev_ref[:, sl] = hg[SUBLANES - 1:, :]

        gb = gate_of(n).astype(F32)
        k = math.sqrt(2.0 / math.pi)
        cdf = 0.5 * (1.0 + jnp.tanh(gb * (k + (k * 0.044715) * (gb * gb))))
        y_ref[:, sl] = (h * (gb * cdf)).astype(y_ref.dtype)


def _rglru_kernel(xb_ref, gb_ref, cw_ref, cb_ref, wa_ref, ba_ref, wx_ref, bx_ref, lam_ref,
                  *refs, n_cast):
    cast_in, y_ref, cast_out = refs[:n_cast], refs[n_cast], refs[n_cast + 1:2 * n_cast + 1]
    xs_ref, as_ref, bs_ref, hprev_ref = refs[2 * n_cast + 1:]
    ts = xb_ref.shape[0]
    _side_cast(cast_in, cast_out)

    @pl.when(pl.program_id(1) == 0)
    def _():
        _rglru_reset(xs_ref, hprev_ref, ts)

    _rglru_tile(lambda n: xb_ref[:, n * HEAD_DIM:(n + 1) * HEAD_DIM],
                lambda n: gb_ref[:, n * HEAD_DIM:(n + 1) * HEAD_DIM],
                y_ref, cw_ref, cb_ref, wa_ref, ba_ref, wx_ref, bx_ref, lam_ref,
                xs_ref, as_ref, bs_ref, hprev_ref, ts)


def _rglru(proj, conv_w, conv_b, wa, ba, wx, bx, lam, layer, casts=(), *, ts=512):
    bsz, s, _ = proj.shape
    c = conv_w.shape[1]
    ts = _row_tile(s, ts)
    nt = s // ts
    vec = lambda v: v.reshape(1, c)
    full2 = lambda shape: pl.BlockSpec(shape, lambda b, i: (0, 0))
    gate_w = lambda w: pl.BlockSpec((None,) + w.shape[1:], lambda b, i: (layer, 0, 0, 0))
    stage = pltpu.VMEM((c // HEAD_DIM, ts + LRU_PAD, HEAD_DIM), F32)
    cast_in, cast_out, cast_shapes = _side_cast_specs(casts, bsz * nt, lambda b, i: b * nt + i)
    out = pl.pallas_call(
        functools.partial(_rglru_kernel, n_cast=len(casts)),
        grid=(bsz, nt),
        in_specs=[
            pl.BlockSpec((None, ts, c), lambda b, i: (b, i, 0)),
            pl.BlockSpec((None, ts, c), lambda b, i: (b, i, 1)),
            full2((CONV_WIDTH, c)), full2((1, c)),
            gate_w(wa), full2((1, c)),
            gate_w(wx), full2((1, c)),
            full2((1, c)),
        ] + cast_in,
        out_specs=[pl.BlockSpec((None, ts, c), lambda b, i: (b, i, 0))] + cast_out,
        out_shape=[jax.ShapeDtypeStruct((bsz, s, c), BF16)] + cast_shapes,
        scratch_shapes=[stage, stage, stage, pltpu.VMEM((1, c), F32)],
        compiler_params=_params("arbitrary", "arbitrary"),
        name="rglru",
    )(proj, proj, conv_w, vec(conv_b), wa, vec(ba), wx, vec(bx), vec(lam),
      *[cw for cw, _ in casts])
    return out[0], out[1:]


def _band_masks(first_block, dil):
    qi = lax.broadcasted_iota(jnp.int32, (Q_BLOCK, 2 * Q_BLOCK), 0)
    ki = lax.broadcasted_iota(jnp.int32, (Q_BLOCK, 2 * Q_BLOCK), 1)
    rel = qi + Q_BLOCK - ki
    valid = (rel >= 0) & (rel <= Q_BLOCK)
    valid_first = valid & (ki >= jnp.where(first_block, Q_BLOCK, 0))
    return valid, valid_first, (rel * dil).astype(F32)


def _alibi_slope(group, hh):
    n_heads = len(DIL_GROUPS) * HEADS_PER_GROUP
    return 2.0 ** (-8.0 * (group * HEADS_PER_GROUP + hh + 1) / n_heads)


def _attend(q, k, v, bias, mask):
    sc = lax.dot_general(q, k, (((1,), (1,)), ((), ())),
                         preferred_element_type=F32) * (HEAD_DIM ** -0.5)
    sc = jnp.where(mask, sc + bias, NEG_INF)
    mx = jnp.max(sc, axis=-1, keepdims=True)
    e = jnp.exp(sc - mx)
    den = jnp.sum(e, axis=-1, keepdims=True)
    out = jnp.dot(e.astype(BF16), v, preferred_element_type=F32) / den
    return out, mx + jnp.log(den)


def _dil_attn_dense_kernel(q_ref, kp_ref, kc_ref, vp_ref, vc_ref, o_ref, l_ref, *, group):
    valid, valid_first, dist = _band_masks(pl.program_id(1) == 0, 1)
    for hh in range(HEADS_PER_GROUP):
        hs = slice(hh * HEAD_DIM, (hh + 1) * HEAD_DIM)
        bias = -_alibi_slope(group, hh) * dist
        for j in range(q_ref.shape[0] // Q_BLOCK):
            cur = slice(j * Q_BLOCK, (j + 1) * Q_BLOCK)
            if j == 0:
                k_prev, v_prev, mask = kp_ref[:, hs], vp_ref[:, hs], valid_first
            else:
                prev = slice((j - 1) * Q_BLOCK, j * Q_BLOCK)
                k_prev, v_prev, mask = kc_ref[prev, hs], vc_ref[prev, hs], valid
            k = jnp.concatenate([k_prev, kc_ref[cur, hs]], axis=0)
            v = jnp.concatenate([v_prev, vc_ref[cur, hs]], axis=0)
            out, lse = _attend(q_ref[cur, hs], k, v, bias, mask)
            o_ref[cur, hs] = out.astype(o_ref.dtype)
            l_ref[cur, hs] = jnp.broadcast_to(lse, (Q_BLOCK, HEAD_DIM))


RESIDUE_RADIX = 4


class _ResidueStage:
    def __init__(self, stage_ref, mid_ref, dil):
        self.stage, self.mid, self.dil = stage_ref, mid_ref, dil
        self.two_hops = dil > RESIDUE_RADIX
        assert dil == RESIDUE_RADIX or dil == RESIDUE_RADIX ** 2

    def _hop_rows(self, low):
        return pl.ds(low, self.stage.shape[0] // RESIDUE_RADIX, stride=RESIDUE_RADIX)

    def load(self, value):
        self.stage[...] = value
        if self.two_hops:
            for low in range(RESIDUE_RADIX):
                self.mid[low] = self.stage[self._hop_rows(low), :]

    def _mid_rows(self, r):
        return pl.ds(r // RESIDUE_RADIX, Q_BLOCK, stride=self.dil // RESIDUE_RADIX)

    def read(self, r):
        if self.two_hops:
            return self.mid[r % RESIDUE_RADIX, self._mid_rows(r), :]
        return self.stage[pl.ds(r, Q_BLOCK, stride=self.dil), :]

    def write(self, r, value):
        if self.two_hops:
            self.mid[r % RESIDUE_RADIX, self._mid_rows(r), :] = value
        else:
            self.stage[pl.ds(r, Q_BLOCK, stride=self.dil), :] = value

    def store(self):
        if self.two_hops:
            for low in range(RESIDUE_RADIX):
                self.stage[self._hop_rows(low), :] = self.mid[low]
        return self.stage[...]


def _dil_attn_strided_kernel(q_ref, k_ref, v_ref, o_ref, l_ref, kd_ref, vd_ref, *stage_refs,
                             dil, group):
    first_span = pl.program_id(1) == 0

    @pl.when(first_span)
    def _():
        kd_ref[...] = jnp.zeros_like(kd_ref)
        vd_ref[...] = jnp.zeros_like(vd_ref)

    refs = list(stage_refs)
    qs, ks, vs, os_, ls = (
        _ResidueStage(refs.pop(0), refs.pop(0) if dil > RESIDUE_RADIX else None, dil)
        for _ in range(5))
    _, valid_first, dist = _band_masks(first_span, dil)
    for hh in range(HEADS_PER_GROUP):
        hs = slice(hh * HEAD_DIM, (hh + 1) * HEAD_DIM)
        bias = -_alibi_slope(group, hh) * dist
        qs.load(q_ref[:, hs].astype(F32))
        ks.load(k_ref[:, hs].astype(F32))
        vs.load(v_ref[:, hs].astype(F32))

        for r in range(dil):
            k_cur = ks.read(r).astype(BF16)
            v_cur = vs.read(r).astype(BF16)
            k = jnp.concatenate([kd_ref[hh, r], k_cur], axis=0)
            v = jnp.concatenate([vd_ref[hh, r], v_cur], axis=0)
            out, lse = _attend(qs.read(r).astype(BF16), k, v, bias, valid_first)
            kd_ref[hh, r] = k_cur
            vd_ref[hh, r] = v_cur
            os_.write(r, out)
            ls.write(r, jnp.broadcast_to(lse, (Q_BLOCK, HEAD_DIM)))
        o_ref[:, hs] = os_.store().astype(o_ref.dtype)
        l_ref[:, hs] = ls.store()


def _dil_attn(proj, kv, group, *, dense_rows=1024):
    _, dil = DIL_GROUPS[group]
    bsz, s, _ = proj.shape
    v_off = kv.shape[2] // (2 * GROUP_WIDTH)
    assert s % (dil * Q_BLOCK) == 0
    if dil == 1:
        rows = _row_tile(s, dense_rows)
        ratio = rows // Q_BLOCK
        prev = lambda col: pl.BlockSpec((None, Q_BLOCK, GROUP_WIDTH),
                                        lambda b, n: (b, jnp.maximum(n * ratio - 1, 0), col))
        body = functools.partial(_dil_attn_dense_kernel, group=group)
        operands, scratch = ("q", "kp", "k", "vp", "v"), []
    else:
        rows = dil * Q_BLOCK
        body = functools.partial(_dil_attn_strided_kernel, dil=dil, group=group)
        operands = ("q", "k", "v")
        split = pltpu.VMEM((HEADS_PER_GROUP, dil, Q_BLOCK, HEAD_DIM), BF16)
        stage = [pltpu.VMEM((rows, HEAD_DIM), F32)]
        if dil > RESIDUE_RADIX:
            stage.append(pltpu.VMEM((RESIDUE_RADIX, rows // RESIDUE_RADIX, HEAD_DIM), F32))
        scratch = [split, split] + stage * 5
    cur = lambda col: pl.BlockSpec((None, rows, GROUP_WIDTH), lambda b, n: (b, n, col))
    specs = {"q": cur(group), "k": cur(group), "v": cur(v_off + group)}
    if dil == 1:
        specs.update(kp=prev(group), vp=prev(v_off + group))
    return pl.pallas_call(
        body,
        grid=(bsz, s // rows),
        in_specs=[specs[name] for name in operands],
        out_specs=[cur(0), cur(0)],
        out_shape=[jax.ShapeDtypeStruct((bsz, s, GROUP_WIDTH), BF16),
                   jax.ShapeDtypeStruct((bsz, s, GROUP_WIDTH), F32)],
        scratch_shapes=scratch,
        compiler_params=_params("arbitrary", "arbitrary"),
        name=f"dil_attn_g{group}",
    )(*[proj if name == "q" else kv for name in operands])


def _mem_attn(q, kv_ref):
    outs = []
    for hh in range(MEM_HEADS):
        hs = slice(hh * HEAD_DIM, (hh + 1) * HEAD_DIM)
        vs = slice(MEM_WIDTH + hh * HEAD_DIM, MEM_WIDTH + (hh + 1) * HEAD_DIM)
        sc = lax.dot_general(q[:, hs], kv_ref[:, hs], (((1,), (1,)), ((), ())),
                             preferred_element_type=F32) * (HEAD_DIM ** -0.5)
        e = jnp.exp(sc - jnp.max(sc, axis=-1, keepdims=True))
        den = jnp.sum(e, axis=-1, keepdims=True)
        outs.append(jnp.dot(e.astype(BF16), kv_ref[:, vs], preferred_element_type=F32) / den)
    return jnp.concatenate(outs, axis=-1)


def _mix_out_tail(mixer, q_ref, kv_ref, w_ref, g_ref, h_ref, o_ref):
    m = _mem_attn(q_ref[...], kv_ref)
    mix = jnp.concatenate([mixer, m.astype(BF16)], axis=-1)
    out = jnp.dot(mix, w_ref[...], preferred_element_type=F32)
    o_ref[...] = h_ref[...] + _rms(out, g_ref[...])


def _mix_out_rec_kernel(y_ref, q_ref, kv_ref, w_ref, g_ref, h_ref, o_ref):
    _mix_out_tail(y_ref[...], q_ref, kv_ref, w_ref, g_ref, h_ref, o_ref)


def _mix_out_dil_kernel(o0_ref, o1_ref, o2_ref, l0_ref, l1_ref, l2_ref,
                        q_ref, kv_ref, w_ref, g_ref, h_ref, o_ref):
    lses = [l0_ref[...], l1_ref[...], l2_ref[...]]
    mx = jnp.maximum(jnp.maximum(lses[0], lses[1]), lses[2])
    es = [jnp.exp(l - mx) for l in lses]
    den = es[0] + es[1] + es[2]
    parts = [(o_ref_g[...].astype(F32) * (e / den)).astype(BF16)
             for o_ref_g, e in zip((o0_ref, o1_ref, o2_ref), es)]
    _mix_out_tail(jnp.concatenate(parts, axis=-1), q_ref, kv_ref, w_ref, g_ref, h_ref, o_ref)


def _mix_out(mixer_inputs, proj, memkv, w_out, post_g, h, layer, *, tm=512):
    bsz, s, d = h.shape
    tm = _row_tile(s, tm)
    q_block = proj.shape[2] // MEM_WIDTH - 1
    row_spec = lambda width, col=0: pl.BlockSpec((None, tm, width), lambda b, i: (b, i, col))
    kernel = _mix_out_rec_kernel if len(mixer_inputs) == 1 else _mix_out_dil_kernel
    return pl.pallas_call(
        kernel,
        grid=(bsz, s // tm),
        in_specs=[row_spec(a.shape[2]) for a in mixer_inputs] + [
            row_spec(MEM_WIDTH, q_block),
            pl.BlockSpec((None,) + memkv.shape[1:], lambda b, i: (b, 0, 0)),
            _resident((None,) + w_out.shape[1:], lambda b, i: (layer, 0, 0)),
            _resident((1, d), lambda b, i: (0, 0)),
            row_spec(d),
        ],
        out_specs=row_spec(d),
        out_shape=jax.ShapeDtypeStruct((bsz, s, d), F32),
        compiler_params=_params("arbitrary", "arbitrary"),
        name="mix_out_rec" if len(mixer_inputs) == 1 else "mix_out_dil",
    )(*mixer_inputs, proj, memkv, w_out, post_g.reshape(1, d), h)


def kernel(x, mem, mem_norm_g, a_pre_mix_g, a_post_mix_g, a_pre_ffn_g, a_post_ffn_g, a_w_in, a_conv_w, a_conv_b, a_gate_a_w, a_gate_a_b, a_gate_x_w, a_gate_x_b, a_lambda, a_w_mem_kv, a_w_out, a_w_ffn_in, a_w_ffn_out, kv_norm_g, w_kv_shared, b_pre_mix_g, b_post_mix_g, b_pre_ffn_g, b_post_ffn_g, b_w_in, b_w_mem_kv, b_w_out, b_w_ffn_in, b_w_ffn_out):
    bsz, s, d = x.shape
    n_mem = mem.shape[1]
    n_a = a_w_in.shape[0]
    n_b = b_w_in.shape[0]
    bf = lambda w: w.astype(BF16)
    a_w_in, a_gate_a_w, a_gate_x_w, a_w_mem_kv, b_w_mem_kv = map(
        bf, (a_w_in, a_gate_a_w, a_gate_x_w, a_w_mem_kv, b_w_mem_kv))

    def flat(t):
        return t.reshape(bsz * s, t.shape[-1])

    def mem_kv(w, l):
        kv_mem = _norm_matmul(mem.reshape(bsz * n_mem, d), mem_norm_g, w, l)
        return kv_mem.reshape(bsz, n_mem, -1)

    def ffn(h, pre_g, post_g, w_in, w_out):
        return _ffn(flat(h), pre_g, post_g, w_in, w_out, 0).reshape(bsz, s, d)

    h = x
    for l in range(n_a):
        proj = _norm_matmul(flat(h), a_pre_mix_g[l], a_w_in, l).reshape(bsz, s, -1)
        casts = [(a_w_ffn_in, l), (a_w_ffn_out, l), (a_w_out, l)]
        if l == n_a - 1:
            casts += [(w_kv_shared[None], 0), (b_w_in.reshape(1, n_b * d, -1), 0)]
        y, cast = _rglru(proj, a_conv_w[l], a_conv_b[l], a_gate_a_w, a_gate_a_b[l],
                         a_gate_x_w, a_gate_x_b[l], a_lambda[l], l, casts)
        h = _mix_out([y], proj, mem_kv(a_w_mem_kv, l), cast[2], a_post_mix_g[l], h, 0)
        h = ffn(h, a_pre_ffn_g[l], a_post_ffn_g[l], cast[0], cast[1])
    w_kv_bf, b_w_in_bf = cast[3], cast[4].reshape(b_w_in.shape)

    kv = _norm_matmul(flat(h), kv_norm_g, w_kv_bf, 0).reshape(bsz, s, -1)
    for l in range(n_b):
        proj, cast = _norm_matmul(flat(h), b_pre_mix_g[l], b_w_in_bf, l,
                                  [(b_w_ffn_in, l), (b_w_ffn_out, l), (b_w_out, l)])
        proj = proj.reshape(bsz, s, -1)
        outs, lses = zip(*[_dil_attn(proj, kv, g) for g in range(len(DIL_GROUPS))])
        h = _mix_out(list(outs) + list(lses), proj, mem_kv(b_w_mem_kv, l), cast[2],
                     b_post_mix_g[l], h, 0)
        h = ffn(h, b_pre_ffn_g[l], b_post_ffn_g[l], cast[0], cast[1])
    return h
```
